```python
import math
import jax, jax.numpy as jnp
from jax import lax
import numpy as np

D_MODEL = 1024
BATCH = 8
SEQ = 4096
DEPTH = 1
DEC_BATCH = 128
DEC_SEQ = 1
PAST_LEN = 16384
PAGE_SIZE = 128

N_Q_HEADS = 8
N_KV_HEADS = 2
Q_PER_KV = N_Q_HEADS // N_KV_HEADS
HEAD_DIM = 64
WINDOW = 128
Q_BLOCK = 128
N_BUCKETS = 32
MAX_DISTANCE = 128
N_DELTA_HEADS = 4
DK = 128
DV = 128
CONV_W = 4
CHUNK = 64
N_KEYS = 128
N_EXPERTS = N_KEYS * N_KEYS
PEER_HEADS = 8
PEER_KEY_DIM = 256
PEER_HALF = PEER_KEY_DIM // 2
PEER_TOPK = 16
PEER_BLOCK = 256
RMS_EPS = 1e-6
N_ADA = 6

ATTN_Q_W = N_Q_HEADS * HEAD_DIM
ATTN_KV_W = N_KV_HEADS * HEAD_DIM
DELTA_QK_W = N_DELTA_HEADS * DK
DELTA_V_W = N_DELTA_HEADS * DV
CONV_DIM = 2 * DELTA_QK_W + DELTA_V_W
IN_WIDTHS = (ATTN_Q_W, ATTN_KV_W, ATTN_KV_W, CONV_DIM, N_DELTA_HEADS, N_DELTA_HEADS, DELTA_V_W, D_MODEL, D_MODEL)
IN_W = sum(IN_WIDTHS)

kernel_name = 'hybrid_swa_gdn_peer_step'


def _split(x, widths):
    points, acc = [], 0
    for w in widths[:-1]:
        acc += w
        points.append(acc)
    return jnp.split(x, points, axis=-1)


def _rmsnorm(x, g):
    xf = x.astype(jnp.float32)
    y = xf * lax.rsqrt(jnp.mean(xf * xf, axis=-1, keepdims=True) + RMS_EPS)
    return (y * g.astype(jnp.float32)).astype(x.dtype)


def _l2norm(x):
    return x * lax.rsqrt(jnp.sum(x * x, axis=-1, keepdims=True) + RMS_EPS)


def _rel_bucket(dist):
    n = jnp.maximum(dist, 0)
    max_exact = N_BUCKETS // 2
    nf = jnp.maximum(n, 1).astype(jnp.float32)
    large = max_exact + (jnp.log(nf / max_exact) / math.log(MAX_DISTANCE / max_exact)
                         * (N_BUCKETS - max_exact)).astype(jnp.int32)
    return jnp.where(n < max_exact, n, jnp.minimum(large, N_BUCKETS - 1))


def _swa_attend(q, k, v, q_pos, k_pos, sinks, rel_bias):
    nb, nq = q_pos.shape
    nk = k_pos.shape[1]
    dist = q_pos[:, :, None] - k_pos[:, None, :]
    valid = (dist >= 0) & (dist < WINDOW) & (k_pos[:, None, :] >= 0)
    bias = rel_bias.astype(jnp.float32)[_rel_bucket(dist)]
    bias = bias.reshape(nb, nq, nk, N_KV_HEADS, Q_PER_KV).transpose(0, 3, 4, 1, 2)
    logits = jnp.einsum('bnqhgd,bnkhd->bnhgqk', q.astype(jnp.float32), k.astype(jnp.float32)) * (HEAD_DIM ** -0.5)
    logits = jnp.where(valid[None, :, None, None], logits + bias[None], -jnp.inf)
    sink = sinks.astype(jnp.float32).reshape(N_KV_HEADS, Q_PER_KV)[None, None, :, :, None, None]
    m = jnp.maximum(jnp.max(logits, axis=-1, keepdims=True), sink)
    p = jnp.exp(logits - m)
    denom = jnp.sum(p, axis=-1, keepdims=True) + jnp.exp(sink - m)
    o = jnp.einsum('bnhgqk,bnkhd->bnqhgd', p / denom, v.astype(jnp.float32))
    return o.astype(q.dtype)


def _swa_prompt(q, k, v, sinks, rel_bias):
    B, L = q.shape[:2]
    nb = L // Q_BLOCK
    qb = q.reshape(B, nb, Q_BLOCK, N_KV_HEADS, Q_PER_KV, HEAD_DIM)
    def band(t):
        tp = jnp.concatenate([jnp.zeros((B, Q_BLOCK, N_KV_HEADS, HEAD_DIM), t.dtype), t], axis=1)
        prev = tp[:, :L].reshape(B, nb, Q_BLOCK, N_KV_HEADS, HEAD_DIM)
        cur = t.reshape(B, nb, Q_BLOCK, N_KV_HEADS, HEAD_DIM)
        return jnp.concatenate([prev, cur], axis=2)
    blk = jnp.arange(nb, dtype=jnp.int32)[:, None]
    q_pos = blk * Q_BLOCK + jnp.arange(Q_BLOCK, dtype=jnp.int32)[None]
    k_pos = (blk - 1) * Q_BLOCK + jnp.arange(2 * Q_BLOCK, dtype=jnp.int32)[None]
    o = _swa_attend(qb, band(k), band(v), q_pos, k_pos, sinks, rel_bias)
    return o.reshape(B, L, N_KV_HEADS, Q_PER_KV, HEAD_DIM)


def _swa_sample(q, k, v, k_buf, v_buf, sinks, rel_bias):
    T = q.shape[1]
    W = k_buf.shape[1]
    kc = jnp.concatenate([k_buf.astype(k.dtype), k], axis=1)
    vc = jnp.concatenate([v_buf.astype(v.dtype), v], axis=1)
    q_pos = PAST_LEN + jnp.arange(T, dtype=jnp.int32)
    k_pos = PAST_LEN - W + jnp.arange(W + T, dtype=jnp.int32)
    o = _swa_attend(q[:, None], kc[:, None], vc[:, None], q_pos[None], k_pos[None], sinks, rel_bias)[:, 0]
    return o, kc[:, -W:], vc[:, -W:]


def _short_conv(u, buf, w):
    L = u.shape[1]
    full = jnp.concatenate([buf.astype(u.dtype), u], axis=1)
    out = full[:, 0:L] * w[0]
    for i in range(1, CONV_W):
        out = out + full[:, i:i + L] * w[i]
    return jax.nn.silu(out), full[:, L:]


def _gated_delta(q, k, v, g, beta, s0):
    B, L, H, _ = q.shape
    C = min(CHUNK, L)
    n = -(-L // C)
    pad = n * C - L
    q = _l2norm(q) * (DK ** -0.5)
    k = _l2norm(k)
    if pad:
        pw = ((0, 0), (0, pad), (0, 0), (0, 0))
        q, k, v = jnp.pad(q, pw), jnp.pad(k, pw), jnp.pad(v, pw)
        g, beta = jnp.pad(g, pw[:3]), jnp.pad(beta, pw[:3])
    def chunks(t):
        t = t.reshape((B, n, C, H) + t.shape[3:])
        return jnp.moveaxis(t, (1, 3), (0, 2))
    qc, kc, vc, bc = chunks(q), chunks(k), chunks(v), chunks(beta)
    gc = jnp.cumsum(chunks(g), axis=-1)
    causal = jnp.tril(jnp.ones((C, C), bool))
    strict = jnp.tril(jnp.ones((C, C), bool), -1)
    diff = gc[..., :, None] - gc[..., None, :]
    decay = jnp.where(causal, jnp.exp(jnp.where(causal, diff, 0.0)), 0.0)
    kb = kc * bc[..., None]
    vb = vc * bc[..., None]
    lmat = jnp.where(strict, jnp.einsum('nbhcd,nbhsd->nbhcs', kb, kc) * decay, 0.0)
    eye = jnp.eye(C, dtype=jnp.float32)
    tmat = lax.linalg.triangular_solve(lmat + eye, jnp.broadcast_to(eye, lmat.shape),
                                       left_side=True, lower=True, unit_diagonal=True)
    u = jnp.einsum('nbhcs,nbhsd->nbhcd', tmat, vb)
    w = jnp.einsum('nbhcs,nbhsd->nbhcd', tmat, kb * jnp.exp(gc)[..., None])
    a_intra = jnp.where(causal, jnp.einsum('nbhcd,nbhsd->nbhcs', qc, kc) * decay, 0.0)
    def step(S, inp):
        q_i, k_i, u_i, w_i, a_i, g_i = inp
        v_new = u_i - jnp.einsum('bhcd,bhde->bhce', w_i, S)
        o = jnp.einsum('bhcd,bhde->bhce', q_i * jnp.exp(g_i)[..., None], S) + jnp.einsum('bhcs,bhse->bhce', a_i, v_new)
        g_last = g_i[..., -1]
        S = S * jnp.exp(g_last)[..., None, None] + jnp.einsum(
            'bhcd,bhce->bhde', k_i * jnp.exp(g_last[..., None] - g_i)[..., None], v_new)
        return S, o
    S, o = lax.scan(step, s0, (qc, kc, u, w, a_intra, gc))
    o = jnp.moveaxis(o, (0, 2), (1, 3)).reshape(B, n * C, H, DV)[:, :L]
    return o, S


def _peer(h, w_query, sub_keys, expert_u, expert_v):
    B, L, D = h.shape
    T = B * L
    blk = min(PEER_BLOCK, T)
    nblk = -(-T // blk)
    xt = jnp.pad(h.reshape(T, D), ((0, nblk * blk - T), (0, 0))).reshape(nblk, blk, D)
    def one_block(xb):
        qry = (xb @ w_query).reshape(blk, PEER_HEADS, 2, PEER_HALF).astype(jnp.float32)
        s = jnp.einsum('thpd,pkd->thpk', qry, sub_keys.astype(jnp.float32))
        sv, si = lax.top_k(s, PEER_TOPK)
        cand = sv[:, :, 0, :, None] + sv[:, :, 1, None, :]
        cidx = si[:, :, 0, :, None] * N_KEYS + si[:, :, 1, None, :]
        best, pos = lax.top_k(cand.reshape(blk, PEER_HEADS, PEER_TOPK * PEER_TOPK), PEER_TOPK)
        eidx = jnp.take_along_axis(cidx.reshape(blk, PEER_HEADS, PEER_TOPK * PEER_TOPK), pos, axis=-1)
        gate = jax.nn.softmax(best, axis=-1)
        u = expert_u[eidx]
        v = expert_v[eidx]
        act = jax.nn.gelu(jnp.einsum('td,thed->the', xb, u), approximate=False)
        return jnp.einsum('the,thed->td', (gate * act.astype(jnp.float32)).astype(v.dtype), v)
    out = lax.map(one_block, xt)
    return out.reshape(nblk * blk, D)[:T].reshape(B, L, D).astype(h.dtype)


def _layer(x, c, k_buf, v_buf, conv_buf, s0, rel_bias, p):
    B, L, _ = x.shape
    ada = jax.nn.silu(c) @ p['w_ada'] + p['b_ada']
    sh1, sc1, gt1, sh2, sc2, gt2 = [a[:, None, :] for a in jnp.split(ada, N_ADA, axis=-1)]
    h = _rmsnorm(x, p['g_pre_mix']) * (1.0 + sc1) + sh1
    proj = h @ p['w_in']
    aq, ak, av, conv_in, a_dec, b_beta, z, g_a, g_b = _split(proj, IN_WIDTHS)
    q = aq.reshape(B, L, N_KV_HEADS, Q_PER_KV, HEAD_DIM)
    k = ak.reshape(B, L, N_KV_HEADS, HEAD_DIM)
    v = av.reshape(B, L, N_KV_HEADS, HEAD_DIM)
    if k_buf is None:
        o_att = _swa_prompt(q, k, v, p['attn_sinks'], rel_bias)
        new_k, new_v = k[:, -WINDOW:], v[:, -WINDOW:]
        conv_buf = jnp.zeros((B, CONV_W - 1, CONV_DIM), conv_in.dtype)
        s0 = jnp.zeros((B, N_DELTA_HEADS, DK, DV), jnp.float32)
    else:
        o_att, new_k, new_v = _swa_sample(q, k, v, k_buf, v_buf, p['attn_sinks'], rel_bias)
    o_att = o_att.reshape(B, L, ATTN_Q_W)
    conv_out, new_conv = _short_conv(conv_in, conv_buf, p['w_conv'])
    dq, dk_, dv_ = _split(conv_out.astype(jnp.float32), (DELTA_QK_W, DELTA_QK_W, DELTA_V_W))
    g_log = -jnp.exp(p['a_log'].astype(jnp.float32)) * jax.nn.softplus(
        a_dec.astype(jnp.float32) + p['dt_bias'].astype(jnp.float32))
    beta = jax.nn.sigmoid(b_beta.astype(jnp.float32))
    o_d, s_new = _gated_delta(dq.reshape(B, L, N_DELTA_HEADS, DK), dk_.reshape(B, L, N_DELTA_HEADS, DK),
                              dv_.reshape(B, L, N_DELTA_HEADS, DV), g_log, beta, s0.astype(jnp.float32))
    o_d = _rmsnorm(o_d, p['delta_norm']) * jax.nn.silu(z.reshape(B, L, N_DELTA_HEADS, DV).astype(jnp.float32))
    o_d = o_d.reshape(B, L, DELTA_V_W).astype(x.dtype)
    merged = jax.nn.sigmoid(g_a) * (o_att @ p['w_att_out']) + jax.nn.sigmoid(g_b) * (o_d @ p['w_delta_out'])
    mix = merged @ p['w_out']
    x = x + gt1 * _rmsnorm(mix, p['g_post_mix'])
    h2 = _rmsnorm(x, p['g_pre_ffn']) * (1.0 + sc2) + sh2
    f = _peer(h2, p['w_query'], p['sub_keys'], p['expert_u'], p['expert_v'])
    x = x + gt2 * _rmsnorm(f, p['g_post_ffn'])
    return x, new_k, new_v, new_conv, s_new.astype(x.dtype)


def setup_inputs(seed: int = 0) -> dict:
    key = jax.random.key(seed)
    keys = iter(jax.random.split(key, 40))
    def nrm(shape, scale):
        return jax.random.normal(next(keys), shape, jnp.float32) * scale
    def gain(shape):
        return 1.0 + nrm(shape, 0.01)
    w_buf = min(WINDOW, PAST_LEN)
    x_prompt = nrm((BATCH, SEQ, D_MODEL), 1.0)
    x_sample = nrm((DEC_BATCH, DEC_SEQ, D_MODEL), 1.0)
    cache_swa_k = nrm((DEPTH, DEC_BATCH, w_buf, N_KV_HEADS, HEAD_DIM), 1.0)
    cache_swa_v = nrm((DEPTH, DEC_BATCH, w_buf, N_KV_HEADS, HEAD_DIM), 1.0)
    state_conv = nrm((DEPTH, DEC_BATCH, CONV_W - 1, CONV_DIM), 1.0)
    state_delta = nrm((DEPTH, DEC_BATCH, N_DELTA_HEADS, DK, DV), DK ** -0.5)
    c_prompt = nrm((BATCH, D_MODEL), 1.0)
    c_sample = nrm((DEC_BATCH, D_MODEL), 1.0)
    rel_bias = nrm((N_BUCKETS, N_Q_HEADS), 0.5)
    w_ada = nrm((DEPTH, D_MODEL, N_ADA * D_MODEL), 0.5 * D_MODEL ** -0.5)
    b_ada = nrm((DEPTH, N_ADA * D_MODEL), 0.01)
    g_pre_mix = gain((DEPTH, D_MODEL))
    g_post_mix = gain((DEPTH, D_MODEL))
    g_pre_ffn = gain((DEPTH, D_MODEL))
    g_post_ffn = gain((DEPTH, D_MODEL))
    w_in = nrm((DEPTH, D_MODEL, IN_W), D_MODEL ** -0.5)
    attn_sinks = nrm((DEPTH, N_Q_HEADS), 0.5)
    w_conv = nrm((DEPTH, CONV_W, CONV_DIM), CONV_W ** -0.5)
    a_log = jnp.log(jax.random.uniform(next(keys), (DEPTH, N_DELTA_HEADS), jnp.float32, 1.0, 16.0))
    dt_bias = nrm((DEPTH, N_DELTA_HEADS), 0.1)
    delta_norm = gain((DEPTH, DV))
    w_att_out = nrm((DEPTH, ATTN_Q_W, D_MODEL), ATTN_Q_W ** -0.5)
    w_delta_out = nrm((DEPTH, DELTA_V_W, D_MODEL), DELTA_V_W ** -0.5)
    w_out = nrm((DEPTH, D_MODEL, D_MODEL), D_MODEL ** -0.5)
    w_query = nrm((DEPTH, D_MODEL, PEER_HEADS * PEER_KEY_DIM), D_MODEL ** -0.5)
    sub_keys = nrm((DEPTH, 2, N_KEYS, PEER_HALF), PEER_HALF ** -0.5)
    expert_u = nrm((DEPTH, N_EXPERTS, D_MODEL), D_MODEL ** -0.5)
    expert_v = nrm((DEPTH, N_EXPERTS, D_MODEL), D_MODEL ** -0.5)
    return {'x_prompt': x_prompt, 'x_sample': x_sample, 'cache_swa_k': cache_swa_k, 'cache_swa_v': cache_swa_v,
            'state_conv': state_conv, 'state_delta': state_delta, 'c_prompt': c_prompt, 'c_sample': c_sample,
            'rel_bias': rel_bias, 'w_ada': w_ada, 'b_ada': b_ada, 'g_pre_mix': g_pre_mix, 'g_post_mix': g_post_mix,
            'g_pre_ffn': g_pre_ffn, 'g_post_ffn': g_post_ffn, 'w_in': w_in, 'attn_sinks': attn_sinks,
            'w_conv': w_conv, 'a_log': a_log, 'dt_bias': dt_bias, 'delta_norm': delta_norm,
            'w_att_out': w_att_out, 'w_delta_out': w_delta_out, 'w_out': w_out, 'w_query': w_query,
            'sub_keys': sub_keys, 'expert_u': expert_u, 'expert_v': expert_v}


def reference(x_prompt, x_sample, cache_swa_k, cache_swa_v, state_conv, state_delta, c_prompt, c_sample,
              rel_bias, w_ada, b_ada, g_pre_mix, g_post_mix, g_pre_ffn, g_post_ffn, w_in, attn_sinks,
              w_conv, a_log, dt_bias, delta_norm, w_att_out, w_delta_out, w_out, w_query, sub_keys,
              expert_u, expert_v):
    yp, ys = x_prompt, x_sample
    kp_l, vp_l, cp_l, sp_l, ks_l, vs_l, cs_l, ss_l = [], [], [], [], [], [], [], []
    for layer in range(DEPTH):
        p = dict(w_ada=w_ada[layer], b_ada=b_ada[layer], g_pre_mix=g_pre_mix[layer], g_post_mix=g_post_mix[layer],
                 g_pre_ffn=g_pre_ffn[layer], g_post_ffn=g_post_ffn[layer], w_in=w_in[layer],
                 attn_sinks=attn_sinks[layer], w_conv=w_conv[layer], a_log=a_log[layer], dt_bias=dt_bias[layer],
                 delta_norm=delta_norm[layer], w_att_out=w_att_out[layer], w_delta_out=w_delta_out[layer],
                 w_out=w_out[layer], w_query=w_query[layer], sub_keys=sub_keys[layer],
                 expert_u=expert_u[layer], expert_v=expert_v[layer])
        yp, kp, vp, cp, sp = _layer(yp, c_prompt, None, None, None, None, rel_bias, p)
        ys, ks, vs, cs, ss = _layer(ys, c_sample, cache_swa_k[layer], cache_swa_v[layer], state_conv[layer],
                                    state_delta[layer], rel_bias, p)
        kp_l.append(kp); vp_l.append(vp); cp_l.append(cp); sp_l.append(sp)
        ks_l.append(ks); vs_l.append(vs); cs_l.append(cs); ss_l.append(ss)
    new_k_prompt = jnp.stack(kp_l)
    new_v_prompt = jnp.stack(vp_l)
    new_conv_prompt = jnp.stack(cp_l)
    new_delta_prompt = jnp.stack(sp_l)
    new_k_sample = jnp.stack(ks_l)
    new_v_sample = jnp.stack(vs_l)
    new_conv_sample = jnp.stack(cs_l)
    new_delta_sample = jnp.stack(ss_l)
    return (yp, ys, new_k_prompt, new_v_prompt, new_conv_prompt, new_delta_prompt,
            new_k_sample, new_v_sample, new_conv_sample, new_delta_sample)
```

```python
import functools
import math

import jax
import jax.numpy as jnp
from jax import lax
from jax.experimental import pallas as pl
from jax.experimental.pallas import tpu as pltpu

F32 = jnp.float32
BF16 = jnp.bfloat16
HI = lax.Precision.HIGHEST

D_MODEL = 1024
N_Q_HEADS = 8
N_KV_HEADS = 2
Q_PER_KV = N_Q_HEADS // N_KV_HEADS
HEAD_DIM = 64
WINDOW = 128
N_BUCKETS = 32
MAX_DISTANCE = 128
N_DELTA_HEADS = 4
DK = 128
DV = 128
CONV_W = 4
CHUNK = 64
N_KEYS = 128
PEER_HEADS = 8
PEER_HALF = 128
PEER_TOPK = 16
RMS_EPS = 1e-6
N_ADA = 6
ATTN_Q_W = N_Q_HEADS * HEAD_DIM
ATTN_KV_W = N_KV_HEADS * HEAD_DIM
ATTN_W = ATTN_Q_W + 2 * ATTN_KV_W
DELTA_W = N_DELTA_HEADS * DK
CONV_DIM = 3 * DELTA_W
N_SEL = PEER_HEADS * PEER_TOPK
LANES = 128
SUBLANES = 8
NEG_BIG = -1e30
VMEM_LIMIT = 56 * 1024 * 1024


def _cparams(*sem):
    return pltpu.CompilerParams(dimension_semantics=sem, vmem_limit_bytes=VMEM_LIMIT)


def _rms(x, g):
    return x * lax.rsqrt(jnp.mean(x * x, axis=-1, keepdims=True) + RMS_EPS) * g


def _silu(x):
    return x * jax.nn.sigmoid(x)


def _dot(a, b, precision=None):
    return jnp.dot(a, b, preferred_element_type=F32, precision=precision)


def _dot_nt(a, b, precision=None):
    return lax.dot_general(a, b, (((1,), (1,)), ((), ())), preferred_element_type=F32, precision=precision)


def _ada_kernel(c_ref, w_ref, b_ref, o_ref):
    o_ref[...] = _dot(_silu(c_ref[...]), w_ref[...], HI) + b_ref[...]


def _ada(c_all, w_ada, b_ada):
    rows = c_all.shape[0]
    n_out = w_ada.shape[1]
    tn = 512
    return pl.pallas_call(
        _ada_kernel,
        grid=(n_out // tn,),
        in_specs=[pl.BlockSpec((rows, D_MODEL), lambda j: (0, 0)),
                  pl.BlockSpec((D_MODEL, tn), lambda j: (0, j)),
                  pl.BlockSpec((1, tn), lambda j: (0, j))],
        out_specs=pl.BlockSpec((rows, tn), lambda j: (0, j)),
        out_shape=jax.ShapeDtypeStruct((rows, n_out), F32),
        compiler_params=_cparams("parallel"),
        name="ada",
    )(c_all, w_ada, b_ada.reshape(1, n_out))


def _mod_spec(rows, chunk, tiles_per_group):
    return pl.BlockSpec((None, rows, D_MODEL), lambda i: (i // tiles_per_group, 0, chunk))


def _inproj_kernel(x_ref, sh_ref, sc_ref, g_ref, wa_ref, wc_ref, ws_ref, wz_ref, wga_ref, wgb_ref,
                   oa_ref, oc_ref, os_ref, oz_ref, oga_ref, ogb_ref):
    h = (_rms(x_ref[...], g_ref[...]) * (1.0 + sc_ref[...]) + sh_ref[...]).astype(BF16)
    for w_ref, o_ref in ((wa_ref, oa_ref), (wc_ref, oc_ref), (ws_ref, os_ref), (wz_ref, oz_ref),
                         (wga_ref, oga_ref), (wgb_ref, ogb_ref)):
        o_ref[...] = _dot(h, w_ref[...])


def _inproj(x, mods, mod_rows, tiles_per_group, tm, g_pre, weights):
    t = x.shape[0]
    widths = [w.shape[1] for w in weights]
    const = lambda i: (0, 0)
    return pl.pallas_call(
        _inproj_kernel,
        grid=(t // tm,),
        in_specs=[pl.BlockSpec((tm, D_MODEL), lambda i: (i, 0)),
                  _mod_spec(mod_rows, 0, tiles_per_group),
                  _mod_spec(mod_rows, 1, tiles_per_group),
                  pl.BlockSpec((1, D_MODEL), const)]
                 + [pl.BlockSpec((D_MODEL, n), const) for n in widths],
        out_specs=[pl.BlockSpec((tm, n), lambda i: (i, 0)) for n in widths],
        out_shape=[jax.ShapeDtypeStruct((t, n), F32) for n in widths],
        compiler_params=_cparams("parallel"),
        name="inproj",
    )(x, mods, mods, g_pre, *weights)


def _rel_bucket(dist):
    n = jnp.maximum(dist, 0)
    max_exact = N_BUCKETS // 2
    nf = jnp.maximum(n, 1).astype(F32)
    large = max_exact + (jnp.log(nf / max_exact) / math.log(MAX_DISTANCE / max_exact)
                         * (N_BUCKETS - max_exact)).astype(jnp.int32)
    return jnp.where(n < max_exact, n, jnp.minimum(large, N_BUCKETS - 1))


def _softmax_sink(s, sink):
    m = jnp.maximum(jnp.max(s, axis=-1, keepdims=True), sink)
    p = jnp.exp(s - m)
    denom = jnp.sum(p, axis=-1, keepdims=True) + jnp.exp(sink - m)
    return p / denom


def _swa_prompt_kernel(sink_ref, q_ref, kp_ref, kc_ref, vp_ref, vc_ref, bias_ref, o_ref):
    q = q_ref[...] * (HEAD_DIM ** -0.5)
    k2 = jnp.concatenate([kp_ref[...], kc_ref[...]], axis=0).astype(BF16)
    v2 = jnp.concatenate([vp_ref[...], vc_ref[...]], axis=0).astype(BF16)
    for h in range(N_Q_HEADS):
        g = h // Q_PER_KV
        qh = q[:, h * HEAD_DIM:(h + 1) * HEAD_DIM].astype(BF16)
        s = _dot_nt(qh, k2[:, g * HEAD_DIM:(g + 1) * HEAD_DIM]) + bias_ref[h]
        p = _softmax_sink(s, sink_ref[h])
        o_ref[:, h * HEAD_DIM:(h + 1) * HEAD_DIM] = _dot(p.astype(BF16), v2[:, g * HEAD_DIM:(g + 1) * HEAD_DIM])


def _swa_prompt(attn, batch, seq, sinks, bias2):
    nb = seq // WINDOW
    qcol = ATTN_Q_W // ATTN_KV_W
    cur = lambda c: (lambda b, j: (b * nb + j, c))
    prev = lambda c: (lambda b, j: (b * nb + jnp.maximum(j - 1, 0), c))
    return pl.pallas_call(
        _swa_prompt_kernel,
        grid=(batch, nb),
        in_specs=[pl.BlockSpec(memory_space=pltpu.SMEM),
                  pl.BlockSpec((WINDOW, ATTN_Q_W), lambda b, j: (b * nb + j, 0)),
                  pl.BlockSpec((WINDOW, ATTN_KV_W), prev(qcol)),
                  pl.BlockSpec((WINDOW, ATTN_KV_W), cur(qcol)),
                  pl.BlockSpec((WINDOW, ATTN_KV_W), prev(qcol + 1)),
                  pl.BlockSpec((WINDOW, ATTN_KV_W), cur(qcol + 1)),
                  pl.BlockSpec((None, N_Q_HEADS, WINDOW, 2 * WINDOW), lambda b, j: (jnp.minimum(j, 1), 0, 0, 0))],
        out_specs=pl.BlockSpec((WINDOW, ATTN_Q_W), lambda b, j: (b * nb + j, 0)),
        out_shape=jax.ShapeDtypeStruct((batch * seq, ATTN_Q_W), F32),
        compiler_params=_cparams("parallel", "parallel"),
        name="swa_prompt",
    )(sinks, attn, attn, attn, attn, attn, bias2)


def _swa_sample_kernel(sink_ref, q_ref, k_ref, v_ref, bias_ref, o_ref):
    q = q_ref[...] * (HEAD_DIM ** -0.5)
    q2 = jnp.concatenate([q, q], axis=-1)
    head = lax.broadcasted_iota(jnp.int32, q2.shape, 1)
    lane = lax.broadcasted_iota(jnp.int32, q2.shape, 2)
    qm = jnp.where(lane // HEAD_DIM == head // Q_PER_KV, q2, 0.0).astype(BF16)
    s = jnp.einsum("shc,sjc->shj", qm, k_ref[...].astype(BF16), preferred_element_type=F32)
    p = _softmax_sink(s + bias_ref[...][None], sink_ref[...][None])
    o2 = jnp.einsum("shj,sjc->shc", p.astype(BF16), v_ref[...].astype(BF16), preferred_element_type=F32)
    head_o = lax.broadcasted_iota(jnp.int32, o_ref.shape, 1)
    o_ref[...] = jnp.where(head_o < Q_PER_KV, o2[:, :, :HEAD_DIM], o2[:, :, HEAD_DIM:])


def _swa_sample(q, k_win, v_win, sinks, bias_s, sb):
    s = q.shape[0]
    return pl.pallas_call(
        _swa_sample_kernel,
        grid=(s // sb,),
        in_specs=[pl.BlockSpec((N_Q_HEADS, 1), lambda i: (0, 0)),
                  pl.BlockSpec((sb, N_Q_HEADS, HEAD_DIM), lambda i: (i, 0, 0)),
                  pl.BlockSpec((sb, WINDOW, ATTN_KV_W), lambda i: (i, 0, 0)),
                  pl.BlockSpec((sb, WINDOW, ATTN_KV_W), lambda i: (i, 0, 0)),
                  pl.BlockSpec((N_Q_HEADS, WINDOW), lambda i: (0, 0))],
        out_specs=pl.BlockSpec((sb, N_Q_HEADS, HEAD_DIM), lambda i: (i, 0, 0)),
        out_shape=jax.ShapeDtypeStruct((s, N_Q_HEADS, HEAD_DIM), F32),
        compiler_params=_cparams("parallel"),
        name="swa_sample",
    )(sinks.reshape(N_Q_HEADS, 1), q, k_win, v_win, bias_s)


def _l2n(x, scale=1.0):
    return x * (lax.rsqrt(jnp.sum(x * x, axis=-1, keepdims=True) + RMS_EPS) * scale)


def _softplus(x):
    return jnp.maximum(x, 0.0) + jnp.log(1.0 + jnp.exp(-jnp.abs(x)))


def _decay_beta(small, par_ref):
    g = -jnp.exp(par_ref[0:1, :]) * _softplus(small + par_ref[1:2, :])
    return g, jax.nn.sigmoid(small)


def _unit_lower_inverse(lm):
    c = lm.shape[0]
    eye = (lax.broadcasted_iota(jnp.int32, (c, c), 0) == lax.broadcasted_iota(jnp.int32, (c, c), 1)).astype(F32)
    inv = eye - lm
    p = lm
    k = 2
    while k < c:
        p = _dot(p, p, HI)
        inv = inv + _dot(inv, p, HI)
        k *= 2
    return inv


def _delta_prompt_kernel(prev_ref, cur_ref, small_ref, z_ref, wconv_ref, par_ref, norm_ref,
                         o_ref, sout_ref, s_ref):
    n = pl.program_id(1)
    c = CHUNK

    @pl.when(n == 0)
    def _():
        s_ref[...] = jnp.zeros_like(s_ref)

    prev = jnp.where(n > 0, prev_ref[...], 0.0)
    full = jnp.concatenate([prev, cur_ref[...]], axis=0)
    w = wconv_ref[...]
    base = SUBLANES - (CONV_W - 1)
    conv = full[base:base + c] * w[0:1]
    for i in range(1, CONV_W):
        conv = conv + full[base + i:base + i + c] * w[i:i + 1]
    conv = _silu(conv)

    g_all, beta_all = _decay_beta(small_ref[...], par_ref)
    ri = lax.broadcasted_iota(jnp.int32, (c, c), 0)
    ci = lax.broadcasted_iota(jnp.int32, (c, c), 1)
    causal = ri >= ci
    strict = ri > ci
    gc_col = _dot(causal.astype(F32), g_all, HI)
    gc_row = _dot(g_all.T, (ri <= ci).astype(F32), HI)

    for h in range(N_DELTA_HEADS):
        lo = h * DK
        q = _l2n(conv[:, lo:lo + DK], DK ** -0.5)
        k = _l2n(conv[:, DELTA_W + lo:DELTA_W + lo + DK])
        v = conv[:, 2 * DELTA_W + lo:2 * DELTA_W + lo + DV]
        gcol = gc_col[:, h:h + 1]
        grow = gc_row[h:h + 1, :]
        beta = beta_all[:, N_DELTA_HEADS + h:N_DELTA_HEADS + h + 1]
        decay = jnp.where(causal, jnp.exp(jnp.where(causal, gcol - grow, 0.0)), 0.0)
        kb = k * beta
        lm = jnp.where(strict, _dot_nt(kb, k, HI) * decay, 0.0)
        tmat = _unit_lower_inverse(lm)
        u = _dot(tmat, v * beta, HI)
        wmat = _dot(tmat, kb * jnp.exp(gcol), HI)
        a_intra = jnp.where(causal, _dot_nt(q, k, HI) * decay, 0.0)
        s_h = s_ref[h]
        v_new = u - _dot(wmat, s_h, HI)
        o = _dot(q * jnp.exp(gcol), s_h, HI) + _dot(a_intra, v_new, HI)
        g_last = gcol[c - 1:c, :]
        s_ref[h] = s_h * jnp.exp(g_last) + _dot((k * jnp.exp(g_last - gcol)).T, v_new, HI)
        o_ref[:, lo:lo + DV] = _rms(o, norm_ref[...]) * _silu(z_ref[:, lo:lo + DV])

    @pl.when(n == pl.num_programs(1) - 1)
    def _():
        sout_ref[...] = s_ref[...]


def _delta_prompt(conv_in, small, z, batch, seq, w_conv, par, norm):
    nc = seq // CHUNK
    per = CHUNK // SUBLANES
    tile = lambda b, n: (b * nc + n, 0)
    const = lambda b, n: (0, 0)
    return pl.pallas_call(
        _delta_prompt_kernel,
        grid=(batch, nc),
        in_specs=[pl.BlockSpec((SUBLANES, CONV_DIM), lambda b, n: (jnp.maximum((b * nc + n) * per - 1, 0), 0)),
                  pl.BlockSpec((CHUNK, CONV_DIM), tile),
                  pl.BlockSpec((CHUNK, LANES), tile),
                  pl.BlockSpec((CHUNK, DELTA_W), tile),
                  pl.BlockSpec((CONV_W, CONV_DIM), const),
                  pl.BlockSpec((SUBLANES, LANES), const),
                  pl.BlockSpec((1, DV), const)],
        out_specs=[pl.BlockSpec((CHUNK, DELTA_W), tile),
                   pl.BlockSpec((None, N_DELTA_HEADS, DK, DV), lambda b, n: (b, 0, 0, 0))],
        out_shape=[jax.ShapeDtypeStruct((batch * seq, DELTA_W), F32),
                   jax.ShapeDtypeStruct((batch, N_DELTA_HEADS, DK, DV), F32)],
        scratch_shapes=[pltpu.VMEM((N_DELTA_HEADS, DK, DV), F32)],
        compiler_params=_cparams("parallel", "arbitrary"),
        name="delta_prompt",
    )(conv_in, conv_in, small, z, w_conv, par, norm)


def _delta_sample_kernel(r0_ref, r1_ref, r2_ref, r3_ref, small_ref, z_ref, s_ref, wconv_ref, par_ref, norm_ref,
                         o_ref, sout_ref):
    w = wconv_ref[...]
    conv = r0_ref[...] * w[0:1] + r1_ref[...] * w[1:2] + r2_ref[...] * w[2:3] + r3_ref[...] * w[3:4]
    conv = _silu(conv)
    g_all, beta_all = _decay_beta(small_ref[...], par_ref)
    sb = conv.shape[0]
    eye = (lax.broadcasted_iota(jnp.int32, (DK, DK), 0) == lax.broadcasted_iota(jnp.int32, (DK, DK), 1)).astype(F32)
    for h in range(N_DELTA_HEADS):
        lo = h * DK
        q = _l2n(conv[:, lo:lo + DK], DK ** -0.5)
        k = _l2n(conv[:, DELTA_W + lo:DELTA_W + lo + DK])
        v = conv[:, 2 * DELTA_W + lo:2 * DELTA_W + lo + DV]
        eg = jnp.exp(g_all[:, h:h + 1])
        beta = beta_all[:, N_DELTA_HEADS + h:N_DELTA_HEADS + h + 1]
        qk = jnp.sum(q * k, axis=-1, keepdims=True)
        k_t = _dot_nt(eye, k, HI)
        rows = []
        for b in range(sb):
            s_b = s_ref[b, h]
            ks = _dot(k[b:b + 1, :], s_b, HI)
            qs = _dot(q[b:b + 1, :], s_b, HI)
            v_new = beta[b:b + 1, :] * (v[b:b + 1, :] - eg[b:b + 1, :] * ks)
            rows.append(eg[b:b + 1, :] * qs + qk[b:b + 1, :] * v_new)
            sout_ref[b, h] = s_b * eg[b:b + 1, :] + k_t[:, b:b + 1] * v_new
        o = jnp.concatenate(rows, axis=0)
        o_ref[:, lo:lo + DV] = _rms(o, norm_ref[...]) * _silu(z_ref[:, lo:lo + DV])


def _delta_sample(rows4, small, z, state, w_conv, par, norm, sb):
    s = small.shape[0]
    tile = lambda i: (i, 0)
    const = lambda i: (0, 0)
    st = pl.BlockSpec((sb, N_DELTA_HEADS, DK, DV), lambda i: (i, 0, 0, 0))
    return pl.pallas_call(
        _delta_sample_kernel,
        grid=(s // sb,),
        in_specs=[pl.BlockSpec((sb, CONV_DIM), tile)] * 4
                 + [pl.BlockSpec((sb, LANES), tile), pl.BlockSpec((sb, DELTA_W), tile), st,
                    pl.BlockSpec((CONV_W, CONV_DIM), const), pl.BlockSpec((SUBLANES, LANES), const),
                    pl.BlockSpec((1, DV), const)],
        out_specs=[pl.BlockSpec((sb, DELTA_W), tile), st],
        out_shape=[jax.ShapeDtypeStruct((s, DELTA_W), F32),
                   jax.ShapeDtypeStruct((s, N_DELTA_HEADS, DK, DV), F32)],
        compiler_params=_cparams("parallel"),
        name="delta_sample",
    )(*rows4, small, z, state, w_conv, par, norm)


def _outproj_kernel(x_ref, oa_ref, od_ref, ga_ref, gb_ref, gt1_ref, sh2_ref, sc2_ref, gpost_ref, gpre_ref,
                    wa_ref, wd_ref, wo_ref, x1_ref, h2_ref):
    att = _dot(oa_ref[...].astype(BF16), wa_ref[...])
    dlt = _dot(od_ref[...].astype(BF16), wd_ref[...])
    merged = jax.nn.sigmoid(ga_ref[...]) * att + jax.nn.sigmoid(gb_ref[...]) * dlt
    mix = _dot(merged.astype(BF16), wo_ref[...])
    x1 = x_ref[...] + gt1_ref[...] * _rms(mix, gpost_ref[...])
    x1_ref[...] = x1
    h2_ref[...] = _rms(x1, gpre_ref[...]) * (1.0 + sc2_ref[...]) + sh2_ref[...]


def _outproj(x, o_att, o_d, ga, gb, mods, mod_rows, tiles_per_group, tm, g_post, g_pre_ffn, wa, wd, wo):
    t = x.shape[0]
    tile = lambda i: (i, 0)
    const = lambda i: (0, 0)
    wide = pl.BlockSpec((tm, D_MODEL), tile)
    half = pl.BlockSpec((tm, ATTN_Q_W), tile)
    return pl.pallas_call(
        _outproj_kernel,
        grid=(t // tm,),
        in_specs=[wide, half, half, wide, wide,
                  _mod_spec(mod_rows, 2, tiles_per_group),
                  _mod_spec(mod_rows, 3, tiles_per_group),
                  _mod_spec(mod_rows, 4, tiles_per_group),
                  pl.BlockSpec((1, D_MODEL), const), pl.BlockSpec((1, D_MODEL), const),
                  pl.BlockSpec((ATTN_Q_W, D_MODEL), const), pl.BlockSpec((DELTA_W, D_MODEL), const),
                  pl.BlockSpec((D_MODEL, D_MODEL), const)],
        out_specs=[wide, wide],
        out_shape=[jax.ShapeDtypeStruct((t, D_MODEL), F32)] * 2,
        compiler_params=_cparams("parallel"),
        name="outproj",
    )(x, o_att, o_d, ga, gb, mods, mods, mods, g_post, g_pre_ffn, wa, wd, wo)


def _top_rows(s, k, payload=None):
    r = s.shape[0]
    rows = lax.broadcasted_iota(jnp.int32, s.shape, 0).astype(F32)
    pick_from = rows if payload is None else payload
    vals, picks = [], []
    for _ in range(k):
        m = jnp.max(s, axis=0, keepdims=True)
        first = jnp.min(jnp.where(s == m, rows, float(r)), axis=0, keepdims=True)
        sel = rows == first
        vals.append(m)
        picks.append(first if payload is None else jnp.sum(jnp.where(sel, pick_from, 0.0), axis=0, keepdims=True))
        s = jnp.where(sel, -jnp.inf, s)
    return jnp.concatenate(vals, axis=0), jnp.concatenate(picks, axis=0)


def _route_kernel(h_ref, wq_ref, keys_ref, eidx_ref, gate_ref):
    q_t = _dot_nt(wq_ref[...], h_ref[...].astype(BF16))
    k0 = keys_ref[0].astype(BF16)
    k1 = keys_ref[1].astype(BF16)
    eidx, gates = [], []
    for h in range(PEER_HEADS):
        lo = h * 2 * PEER_HALF
        s0 = _dot(k0, q_t[lo:lo + PEER_HALF].astype(BF16))
        s1 = _dot(k1, q_t[lo + PEER_HALF:lo + 2 * PEER_HALF].astype(BF16))
        v0, i0 = _top_rows(s0, PEER_TOPK)
        v1, i1 = _top_rows(s1, PEER_TOPK)
        cand = jnp.concatenate([v0[a:a + 1] + v1 for a in range(PEER_TOPK)], axis=0)
        cidx = jnp.concatenate([i0[a:a + 1] * float(N_KEYS) + i1 for a in range(PEER_TOPK)], axis=0)
        best, e = _top_rows(cand, PEER_TOPK, cidx)
        p = jnp.exp(best - best[0:1])
        gates.append(p / jnp.sum(p, axis=0, keepdims=True))
        eidx.append(e)
    eidx_ref[...] = jnp.concatenate(eidx, axis=0).T.astype(jnp.int32)
    gate_ref[...] = jnp.concatenate(gates, axis=0).T


def _route(h2, wq_t, sub_keys, tm):
    t = h2.shape[0]
    return pl.pallas_call(
        _route_kernel,
        grid=(t // tm,),
        in_specs=[pl.BlockSpec((tm, D_MODEL), lambda i: (i, 0)),
                  pl.BlockSpec(wq_t.shape, lambda i: (0, 0)),
                  pl.BlockSpec(sub_keys.shape, lambda i: (0, 0, 0))],
        out_specs=[pl.BlockSpec((tm, N_SEL), lambda i: (i, 0))] * 2,
        out_shape=[jax.ShapeDtypeStruct((t, N_SEL), jnp.int32), jax.ShapeDtypeStruct((t, N_SEL), F32)],
        compiler_params=_cparams("parallel"),
        name="route",
    )(h2, wq_t, sub_keys)


def _gelu(x):
    return 0.5 * x * (1.0 + lax.erf(x * (2.0 ** -0.5)))


def _peer_kernel(idx0_ref, idxn_ref, gate_ref, h_ref, x_ref, gt_ref, g_ref, uv_ref, y_ref, buf_ref, f_ref, sem_ref,
                 *, tt):
    i = pl.program_id(0)
    n = pl.num_programs(0)
    slot = i % 2
    rows = tt * N_SEL

    def issue(idx_ref, dst_slot):
        def body(t, carry):
            for p in range(N_SEL):
                e = idx_ref[t, p]
                pltpu.make_async_copy(uv_ref.at[pl.ds(e, 1), :],
                                      buf_ref.at[dst_slot, pl.ds(t * N_SEL + p, 1), :],
                                      sem_ref.at[dst_slot]).start()
            return carry
        lax.fori_loop(0, tt, body, 0)

    @pl.when(i == 0)
    def _():
        issue(idx0_ref, 0)

    @pl.when(i + 1 < n)
    def _():
        issue(idxn_ref, 1 - slot)

    pltpu.make_async_copy(uv_ref.at[pl.ds(0, rows), :], buf_ref.at[slot], sem_ref.at[slot]).wait()

    eye = (lax.broadcasted_iota(jnp.int32, (N_SEL, N_SEL), 0)
           == lax.broadcasted_iota(jnp.int32, (N_SEL, N_SEL), 1)).astype(F32)
    gate_t = _dot_nt(eye, gate_ref[...], HI)
    lane = lax.broadcasted_iota(jnp.int32, (N_SEL, tt), 1)
    hmat = jnp.zeros((N_SEL, tt), F32)
    for t in range(tt):
        u = buf_ref[slot, t * N_SEL:(t + 1) * N_SEL, :D_MODEL]
        hcol = jnp.sum(u * h_ref[t:t + 1, :], axis=1, keepdims=True)
        hmat = jnp.where(lane == t, hcol, hmat)
    coef = gate_t * _gelu(hmat)
    for t in range(tt):
        v = buf_ref[slot, t * N_SEL:(t + 1) * N_SEL, D_MODEL:]
        f_ref[t:t + 1, :] = jnp.sum(coef[:, t:t + 1] * v, axis=0, keepdims=True)
    y_ref[...] = x_ref[...] + gt_ref[...] * _rms(f_ref[...], g_ref[...])


def _peer(eidx, gate, h2, x1, mods, mod_rows, tiles_per_group, tt, g_post, uv):
    t = h2.shape[0]
    n = t // tt
    tile = lambda i: (i, 0)
    return pl.pallas_call(
        functools.partial(_peer_kernel, tt=tt),
        grid=(n,),
        in_specs=[pl.BlockSpec((tt, N_SEL), lambda i: (0, 0), memory_space=pltpu.SMEM),
                  pl.BlockSpec((tt, N_SEL), lambda i: (jnp.minimum(i + 1, n - 1), 0), memory_space=pltpu.SMEM),
                  pl.BlockSpec((tt, N_SEL), tile),
                  pl.BlockSpec((tt, D_MODEL), tile),
                  pl.BlockSpec((tt, D_MODEL), tile),
                  _mod_spec(mod_rows, 5, tiles_per_group),
                  pl.BlockSpec((1, D_MODEL), lambda i: (0, 0)),
                  pl.BlockSpec(memory_space=pl.ANY)],
        out_specs=pl.BlockSpec((tt, D_MODEL), tile),
        out_shape=jax.ShapeDtypeStruct((t, D_MODEL), F32),
        scratch_shapes=[pltpu.VMEM((2, tt * N_SEL, 2 * D_MODEL), F32),
                        pltpu.VMEM((tt, D_MODEL), F32),
                        pltpu.SemaphoreType.DMA((2,))],
        compiler_params=_cparams("arbitrary"),
        name="peer",
    )(eidx, eidx, gate, h2, x1, mods, g_post, uv)


def _token_path(x, mods, mod_rows, tiles_per_group, tm, tt, mixer, p):
    attn, conv_in, small, z, ga, gb = _inproj(x, mods, mod_rows, tiles_per_group, tm, p["g_pre_mix"], p["w_in_parts"])
    o_att, o_d, extras = mixer(attn, conv_in, small, z)
    x1, h2 = _outproj(x, o_att, o_d, ga, gb, mods, mod_rows, tiles_per_group, tm, p["g_post_mix"], p["g_pre_ffn"],
                      p["w_att_out"], p["w_delta_out"], p["w_out"])
    eidx, gate = _route(h2, p["wq_t"], p["sub_keys"], tm)
    if mod_rows == 1:
        y = _peer(eidx, gate, h2, x1, mods, 1, tiles_per_group * tm // tt, tt, p["g_post_ffn"], p["uv"])
    else:
        y = _peer(eidx, gate, h2, x1, mods.reshape(-1, tt, N_ADA * D_MODEL), tt, 1, tt, p["g_post_ffn"], p["uv"])
    return y, extras


def _bias_tables(rel_bias):
    qi = jnp.arange(WINDOW, dtype=jnp.int32)[:, None]
    kj = jnp.arange(2 * WINDOW, dtype=jnp.int32)[None, :]
    dist = qi + WINDOW - kj
    valid = (dist >= 0) & (dist < WINDOW)
    bias = jnp.moveaxis(rel_bias[_rel_bucket(dist)], -1, 0)
    later = jnp.where(valid[None], bias, NEG_BIG)
    first = jnp.where((valid & (kj >= WINDOW))[None], bias, NEG_BIG)
    dist_s = WINDOW - 1 - jnp.arange(WINDOW, dtype=jnp.int32)
    bias_s = rel_bias[_rel_bucket(dist_s)].T
    return jnp.stack([first, later]), bias_s


def _layer(xp, xs, c_all, k_buf, v_buf, conv_buf, s0, rel_bias, lp):
    batch, seq, _ = xp.shape
    dec = xs.shape[0]
    assert xs.shape[1] == 1 and seq % WINDOW == 0 and k_buf.shape[1] == WINDOW

    ada = _ada(c_all, lp["w_ada"], lp["b_ada"])
    mods_p = ada[:batch].reshape(batch, 1, N_ADA * D_MODEL)
    mods_s = ada[batch:].reshape(1, dec, N_ADA * D_MODEL)

    w_in = lp["w_in"]
    cuts = (0, ATTN_W, ATTN_W + CONV_DIM, ATTN_W + CONV_DIM + 2 * N_DELTA_HEADS)
    cuts = cuts + (cuts[-1] + DELTA_W, cuts[-1] + DELTA_W + D_MODEL, cuts[-1] + DELTA_W + 2 * D_MODEL)
    parts = [w_in[:, a:b] for a, b in zip(cuts[:-1], cuts[1:])]
    parts[2] = jnp.pad(parts[2], ((0, 0), (0, LANES - 2 * N_DELTA_HEADS)))
    row = lambda v: v.reshape(1, -1)
    par = jnp.zeros((SUBLANES, LANES), F32).at[0, :N_DELTA_HEADS].set(lp["a_log"]).at[1, :N_DELTA_HEADS].set(lp["dt_bias"])
    p = dict(
        w_in_parts=[w.astype(BF16) for w in parts],
        g_pre_mix=row(lp["g_pre_mix"]), g_post_mix=row(lp["g_post_mix"]),
        g_pre_ffn=row(lp["g_pre_ffn"]), g_post_ffn=row(lp["g_post_ffn"]),
        w_att_out=lp["w_att_out"].astype(BF16), w_delta_out=lp["w_delta_out"].astype(BF16),
        w_out=lp["w_out"].astype(BF16),
        wq_t=lp["w_query"].T.astype(BF16), sub_keys=lp["sub_keys"],
        uv=jnp.concatenate([lp["expert_u"], lp["expert_v"]], axis=1),
    )
    norm = row(lp["delta_norm"])
    bias2, bias_s = _bias_tables(rel_bias)
    out = {}

    def mixer_prompt(attn, conv_in, small, z):
        o_att = _swa_prompt(attn, batch, seq, lp["attn_sinks"], bias2)
        o_d, s_new = _delta_prompt(conv_in, small, z, batch, seq, lp["w_conv"], par, norm)
        a3 = attn.reshape(batch, seq, ATTN_W)
        new_k = a3[:, -WINDOW:, ATTN_Q_W:ATTN_Q_W + ATTN_KV_W].reshape(batch, WINDOW, N_KV_HEADS, HEAD_DIM)
        new_v = a3[:, -WINDOW:, ATTN_Q_W + ATTN_KV_W:].reshape(batch, WINDOW, N_KV_HEADS, HEAD_DIM)
        new_conv = conv_in.reshape(batch, seq, CONV_DIM)[:, -(CONV_W - 1):]
        return o_att, o_d, (new_k, new_v, new_conv, s_new)

    def mixer_sample(attn, conv_in, small, z):
        k_new = attn[:, ATTN_Q_W:ATTN_Q_W + ATTN_KV_W].reshape(dec, 1, N_KV_HEADS, HEAD_DIM)
        v_new = attn[:, ATTN_Q_W + ATTN_KV_W:].reshape(dec, 1, N_KV_HEADS, HEAD_DIM)
        new_k = jnp.concatenate([k_buf[:, 1:], k_new], axis=1)
        new_v = jnp.concatenate([v_buf[:, 1:], v_new], axis=1)
        sb = SUBLANES if dec % SUBLANES == 0 else dec
        o_att = _swa_sample(attn[:, :ATTN_Q_W].reshape(dec, N_Q_HEADS, HEAD_DIM),
                            new_k.reshape(dec, WINDOW, ATTN_KV_W), new_v.reshape(dec, WINDOW, ATTN_KV_W),
                            lp["attn_sinks"], bias_s, sb).reshape(dec, ATTN_Q_W)
        rows4 = [conv_buf[:, i] for i in range(CONV_W - 1)] + [conv_in]
        o_d, s_new = _delta_sample(rows4, small, z, s0, lp["w_conv"], par, norm, sb)
        new_conv = jnp.concatenate([conv_buf[:, 1:], conv_in[:, None]], axis=1)
        return o_att, o_d, (new_k, new_v, new_conv, s_new)

    tm_p = 256 if seq % 256 == 0 else WINDOW
    yp, ex_p = _token_path(xp.reshape(batch * seq, D_MODEL), mods_p, 1, seq // tm_p, tm_p, SUBLANES, mixer_prompt, p)
    tm_s = dec
    ys, ex_s = _token_path(xs.reshape(dec, D_MODEL), mods_s, dec, 1, tm_s, SUBLANES, mixer_sample, p)
    return yp.reshape(batch, seq, D_MODEL), ys.reshape(dec, 1, D_MODEL), ex_p, ex_s


def kernel(x_prompt, x_sample, cache_swa_k, cache_swa_v, state_conv, state_delta, c_prompt, c_sample, rel_bias, w_ada, b_ada, g_pre_mix, g_post_mix, g_pre_ffn, g_post_ffn, w_in, attn_sinks, w_conv, a_log, dt_bias, delta_norm, w_att_out, w_delta_out, w_out, w_query, sub_keys, expert_u, expert_v):
    depth = w_in.shape[0]
    yp, ys = x_prompt, x_sample
    c_all = jnp.concatenate([c_prompt, c_sample], axis=0)
    ex_p_all, ex_s_all = [], []
    for layer in range(depth):
        lp = dict(w_ada=w_ada[layer], b_ada=b_ada[layer], g_pre_mix=g_pre_mix[layer], g_post_mix=g_post_mix[layer],
                  g_pre_ffn=g_pre_ffn[layer], g_post_ffn=g_post_ffn[layer], w_in=w_in[layer],
                  attn_sinks=attn_sinks[layer], w_conv=w_conv[layer], a_log=a_log[layer], dt_bias=dt_bias[layer],
                  delta_norm=delta_norm[layer], w_att_out=w_att_out[layer], w_delta_out=w_delta_out[layer],
                  w_out=w_out[layer], w_query=w_query[layer], sub_keys=sub_keys[layer],
                  expert_u=expert_u[layer], expert_v=expert_v[layer])
        yp, ys, ex_p, ex_s = _layer(yp, ys, c_all, cache_swa_k[layer], cache_swa_v[layer], state_conv[layer],
                                    state_delta[layer], rel_bias, lp)
        ex_p_all.append(ex_p)
        ex_s_all.append(ex_s)
    stack = lambda exs, j: jnp.stack([e[j] for e in exs])
    return (yp, ys, stack(ex_p_all, 0), stack(ex_p_all, 1), stack(ex_p_all, 2), stack(ex_p_all, 3),
            stack(ex_s_all, 0), stack(ex_s_all, 1), stack(ex_s_all, 2), stack(ex_s_all, 3))
```

```python
import functools
import math

import jax
import jax.numpy as jnp
from jax import lax
from jax.experimental import pallas as pl
from jax.experimental.pallas import tpu as pltpu

F32 = jnp.float32
BF16 = jnp.bfloat16
HI = lax.Precision.HIGHEST

D_MODEL = 1024
N_Q_HEADS = 8
N_KV_HEADS = 2
Q_PER_KV = N_Q_HEADS // N_KV_HEADS
HEAD_DIM = 64
WINDOW = 128
N_BUCKETS = 32
MAX_DISTANCE = 128
N_DELTA_HEADS = 4
DK = 128
DV = 128
CONV_W = 4
CHUNK = 64
N_KEYS = 128
PEER_HEADS = 8
PEER_HALF = 128
PEER_TOPK = 16
RMS_EPS = 1e-6
N_ADA = 6
ATTN_Q_W = N_Q_HEADS * HEAD_DIM
ATTN_KV_W = N_KV_HEADS * HEAD_DIM
ATTN_W = ATTN_Q_W + 2 * ATTN_KV_W
DELTA_W = N_DELTA_HEADS * DK
CONV_DIM = 3 * DELTA_W
N_SEL = PEER_HEADS * PEER_TOPK
LANES = 128
SUBLANES = 8
NEG_BIG = -1e30
VMEM_LIMIT = 56 * 1024 * 1024


def _cparams(*sem):
    return pltpu.CompilerParams(dimension_semantics=sem, vmem_limit_bytes=VMEM_LIMIT)


def _rms(x, g):
    return x * lax.rsqrt(jnp.mean(x * x, axis=-1, keepdims=True) + RMS_EPS) * g


def _silu(x):
    return x * jax.nn.sigmoid(x)


def _dot(a, b, precision=None):
    return jnp.dot(a, b, preferred_element_type=F32, precision=precision)


def _dot_nt(a, b, precision=None):
    return lax.dot_general(a, b, (((1,), (1,)), ((), ())), preferred_element_type=F32, precision=precision)


def _ada_kernel(c_ref, w_ref, b_ref, o_ref):
    o_ref[...] = _dot(_silu(c_ref[...]), w_ref[...], HI) + b_ref[...]


def _ada(c_all, w_ada, b_ada):
    rows = c_all.shape[0]
    n_out = w_ada.shape[1]
    tn = 512
    return pl.pallas_call(
        _ada_kernel,
        grid=(n_out // tn,),
        in_specs=[pl.BlockSpec((rows, D_MODEL), lambda j: (0, 0)),
                  pl.BlockSpec((D_MODEL, tn), lambda j: (0, j)),
                  pl.BlockSpec((1, tn), lambda j: (0, j))],
        out_specs=pl.BlockSpec((rows, tn), lambda j: (0, j)),
        out_shape=jax.ShapeDtypeStruct((rows, n_out), F32),
        compiler_params=_cparams("parallel"),
        name="ada",
    )(c_all, w_ada, b_ada.reshape(1, n_out))


def _mod_spec(rows, chunk, tiles_per_group):
    return pl.BlockSpec((None, rows, D_MODEL), lambda i: (i // tiles_per_group, 0, chunk))


def _inproj_kernel(x_ref, sh_ref, sc_ref, g_ref, wa_ref, wc_ref, ws_ref, wz_ref, wga_ref, wgb_ref,
                   oa_ref, oc_ref, os_ref, oz_ref, oga_ref, ogb_ref):
    h = (_rms(x_ref[...], g_ref[...]) * (1.0 + sc_ref[...]) + sh_ref[...]).astype(BF16)
    for w_ref, o_ref in ((wa_ref, oa_ref), (wc_ref, oc_ref), (ws_ref, os_ref), (wz_ref, oz_ref),
                         (wga_ref, oga_ref), (wgb_ref, ogb_ref)):
        o_ref[...] = _dot(h, w_ref[...])


def _inproj(x, mods, mod_rows, tiles_per_group, tm, g_pre, weights):
    t = x.shape[0]
    widths = [w.shape[1] for w in weights]
    const = lambda i: (0, 0)
    return pl.pallas_call(
        _inproj_kernel,
        grid=(t // tm,),
        in_specs=[pl.BlockSpec((tm, D_MODEL), lambda i: (i, 0)),
                  _mod_spec(mod_rows, 0, tiles_per_group),
                  _mod_spec(mod_rows, 1, tiles_per_group),
                  pl.BlockSpec((1, D_MODEL), const)]
                 + [pl.BlockSpec((D_MODEL, n), const) for n in widths],
        out_specs=[pl.BlockSpec((tm, n), lambda i: (i, 0)) for n in widths],
        out_shape=[jax.ShapeDtypeStruct((t, n), F32) for n in widths],
        compiler_params=_cparams("parallel"),
        name="inproj",
    )(x, mods, mods, g_pre, *weights)


def _rel_bucket(dist):
    n = jnp.maximum(dist, 0)
    max_exact = N_BUCKETS // 2
    nf = jnp.maximum(n, 1).astype(F32)
    large = max_exact + (jnp.log(nf / max_exact) / math.log(MAX_DISTANCE / max_exact)
                         * (N_BUCKETS - max_exact)).astype(jnp.int32)
    return jnp.where(n < max_exact, n, jnp.minimum(large, N_BUCKETS - 1))


def _softmax_sink(s, sink):
    m = jnp.maximum(jnp.max(s, axis=-1, keepdims=True), sink)
    p = jnp.exp(s - m)
    denom = jnp.sum(p, axis=-1, keepdims=True) + jnp.exp(sink - m)
    return p / denom


def _swa_prompt_kernel(sink_ref, q_ref, kp_ref, kc_ref, vp_ref, vc_ref, bias_ref, o_ref):
    q = q_ref[...] * (HEAD_DIM ** -0.5)
    k2 = jnp.concatenate([kp_ref[...], kc_ref[...]], axis=0).astype(BF16)
    v2 = jnp.concatenate([vp_ref[...], vc_ref[...]], axis=0).astype(BF16)
    for h in range(N_Q_HEADS):
        g = h // Q_PER_KV
        qh = q[:, h * HEAD_DIM:(h + 1) * HEAD_DIM].astype(BF16)
        s = _dot_nt(qh, k2[:, g * HEAD_DIM:(g + 1) * HEAD_DIM]) + bias_ref[h]
        p = _softmax_sink(s, sink_ref[h])
        o_ref[:, h * HEAD_DIM:(h + 1) * HEAD_DIM] = _dot(p.astype(BF16), v2[:, g * HEAD_DIM:(g + 1) * HEAD_DIM])


def _swa_prompt(attn, batch, seq, sinks, bias2):
    nb = seq // WINDOW
    qcol = ATTN_Q_W // ATTN_KV_W
    cur = lambda c: (lambda b, j: (b * nb + j, c))
    prev = lambda c: (lambda b, j: (b * nb + jnp.maximum(j - 1, 0), c))
    return pl.pallas_call(
        _swa_prompt_kernel,
        grid=(batch, nb),
        in_specs=[pl.BlockSpec(memory_space=pltpu.SMEM),
                  pl.BlockSpec((WINDOW, ATTN_Q_W), lambda b, j: (b * nb + j, 0)),
                  pl.BlockSpec((WINDOW, ATTN_KV_W), prev(qcol)),
                  pl.BlockSpec((WINDOW, ATTN_KV_W), cur(qcol)),
                  pl.BlockSpec((WINDOW, ATTN_KV_W), prev(qcol + 1)),
                  pl.BlockSpec((WINDOW, ATTN_KV_W), cur(qcol + 1)),
                  pl.BlockSpec((None, N_Q_HEADS, WINDOW, 2 * WINDOW), lambda b, j: (jnp.minimum(j, 1), 0, 0, 0))],
        out_specs=pl.BlockSpec((WINDOW, ATTN_Q_W), lambda b, j: (b * nb + j, 0)),
        out_shape=jax.ShapeDtypeStruct((batch * seq, ATTN_Q_W), F32),
        compiler_params=_cparams("parallel", "parallel"),
        name="swa_prompt",
    )(sinks, attn, attn, attn, attn, attn, bias2)


def _swa_sample_kernel(sink_ref, q_ref, k_ref, v_ref, bias_ref, o_ref):
    q = q_ref[...] * (HEAD_DIM ** -0.5)
    q2 = jnp.concatenate([q, q], axis=-1)
    head = lax.broadcasted_iota(jnp.int32, q2.shape, 1)
    lane = lax.broadcasted_iota(jnp.int32, q2.shape, 2)
    qm = jnp.where(lane // HEAD_DIM == head // Q_PER_KV, q2, 0.0).astype(BF16)
    s = jnp.einsum("shc,sjc->shj", qm, k_ref[...].astype(BF16), preferred_element_type=F32)
    p = _softmax_sink(s + bias_ref[...][None], sink_ref[...][None])
    o2 = jnp.einsum("shj,sjc->shc", p.astype(BF16), v_ref[...].astype(BF16), preferred_element_type=F32)
    head_o = lax.broadcasted_iota(jnp.int32, o_ref.shape, 1)
    o_ref[...] = jnp.where(head_o < Q_PER_KV, o2[:, :, :HEAD_DIM], o2[:, :, HEAD_DIM:])


def _swa_sample(q, k_win, v_win, sinks, bias_s, sb):
    s = q.shape[0]
    return pl.pallas_call(
        _swa_sample_kernel,
        grid=(s // sb,),
        in_specs=[pl.BlockSpec((N_Q_HEADS, 1), lambda i: (0, 0)),
                  pl.BlockSpec((sb, N_Q_HEADS, HEAD_DIM), lambda i: (i, 0, 0)),
                  pl.BlockSpec((sb, WINDOW, ATTN_KV_W), lambda i: (i, 0, 0)),
                  pl.BlockSpec((sb, WINDOW, ATTN_KV_W), lambda i: (i, 0, 0)),
                  pl.BlockSpec((N_Q_HEADS, WINDOW), lambda i: (0, 0))],
        out_specs=pl.BlockSpec((sb, N_Q_HEADS, HEAD_DIM), lambda i: (i, 0, 0)),
        out_shape=jax.ShapeDtypeStruct((s, N_Q_HEADS, HEAD_DIM), F32),
        compiler_params=_cparams("parallel"),
        name="swa_sample",
    )(sinks.reshape(N_Q_HEADS, 1), q, k_win, v_win, bias_s)


def _l2n(x, scale=1.0):
    return x * (lax.rsqrt(jnp.sum(x * x, axis=-1, keepdims=True) + RMS_EPS) * scale)


def _softplus(x):
    return jnp.maximum(x, 0.0) + jnp.log(1.0 + jnp.exp(-jnp.abs(x)))


def _decay_beta(small, par_ref):
    g = -jnp.exp(par_ref[0:1, :]) * _softplus(small + par_ref[1:2, :])
    return g, jax.nn.sigmoid(small)


def _unit_lower_inverse(lm):
    c = lm.shape[0]
    eye = (lax.broadcasted_iota(jnp.int32, (c, c), 0) == lax.broadcasted_iota(jnp.int32, (c, c), 1)).astype(F32)
    inv = eye - lm
    p = lm
    k = 2
    while k < c:
        p = _dot(p, p, HI)
        inv = inv + _dot(inv, p, HI)
        k *= 2
    return inv


def _delta_prompt_kernel(prev_ref, cur_ref, small_ref, z_ref, wconv_ref, par_ref, norm_ref,
                         o_ref, sout_ref, s_ref):
    n = pl.program_id(1)
    c = CHUNK

    @pl.when(n == 0)
    def _():
        s_ref[...] = jnp.zeros_like(s_ref)

    prev = jnp.where(n > 0, prev_ref[...], 0.0)
    full = jnp.concatenate([prev, cur_ref[...]], axis=0)
    w = wconv_ref[...]
    base = SUBLANES - (CONV_W - 1)
    conv = full[base:base + c] * w[0:1]
    for i in range(1, CONV_W):
        conv = conv + full[base + i:base + i + c] * w[i:i + 1]
    conv = _silu(conv)

    g_all, beta_all = _decay_beta(small_ref[...], par_ref)
    ri = lax.broadcasted_iota(jnp.int32, (c, c), 0)
    ci = lax.broadcasted_iota(jnp.int32, (c, c), 1)
    causal = ri >= ci
    strict = ri > ci
    gc_col = _dot(causal.astype(F32), g_all, HI)
    gc_row = _dot(g_all.T, (ri <= ci).astype(F32), HI)

    for h in range(N_DELTA_HEADS):
        lo = h * DK
        q = _l2n(conv[:, lo:lo + DK], DK ** -0.5)
        k = _l2n(conv[:, DELTA_W + lo:DELTA_W + lo + DK])
        v = conv[:, 2 * DELTA_W + lo:2 * DELTA_W + lo + DV]
        gcol = gc_col[:, h:h + 1]
        grow = gc_row[h:h + 1, :]
        beta = beta_all[:, N_DELTA_HEADS + h:N_DELTA_HEADS + h + 1]
        decay = jnp.where(causal, jnp.exp(jnp.where(causal, gcol - grow, 0.0)), 0.0)
        kb = k * beta
        lm = jnp.where(strict, _dot_nt(kb, k, HI) * decay, 0.0)
        tmat = _unit_lower_inverse(lm)
        u = _dot(tmat, v * beta, HI)
        wmat = _dot(tmat, kb * jnp.exp(gcol), HI)
        a_intra = jnp.where(causal, _dot_nt(q, k, HI) * decay, 0.0)
        s_h = s_ref[h]
        v_new = u - _dot(wmat, s_h, HI)
        o = _dot(q * jnp.exp(gcol), s_h, HI) + _dot(a_intra, v_new, HI)
        g_last = gcol[c - 1:c, :]
        s_ref[h] = s_h * jnp.exp(g_last) + _dot((k * jnp.exp(g_last - gcol)).T, v_new, HI)
        o_ref[:, lo:lo + DV] = _rms(o, norm_ref[...]) * _silu(z_ref[:, lo:lo + DV])

    @pl.when(n == pl.num_programs(1) - 1)
    def _():
        sout_ref[...] = s_ref[...]


def _delta_prompt(conv_in, small, z, batch, seq, w_conv, par, norm):
    nc = seq // CHUNK
    per = CHUNK // SUBLANES
    tile = lambda b, n: (b * nc + n, 0)
    const = lambda b, n: (0, 0)
    return pl.pallas_call(
        _delta_prompt_kernel,
        grid=(batch, nc),
        in_specs=[pl.BlockSpec((SUBLANES, CONV_DIM), lambda b, n: (jnp.maximum((b * nc + n) * per - 1, 0), 0)),
                  pl.BlockSpec((CHUNK, CONV_DIM), tile),
                  pl.BlockSpec((CHUNK, LANES), tile),
                  pl.BlockSpec((CHUNK, DELTA_W), tile),
                  pl.BlockSpec((CONV_W, CONV_DIM), const),
                  pl.BlockSpec((SUBLANES, LANES), const),
                  pl.BlockSpec((1, DV), const)],
        out_specs=[pl.BlockSpec((CHUNK, DELTA_W), tile),
                   pl.BlockSpec((None, N_DELTA_HEADS, DK, DV), lambda b, n: (b, 0, 0, 0))],
        out_shape=[jax.ShapeDtypeStruct((batch * seq, DELTA_W), F32),
                   jax.ShapeDtypeStruct((batch, N_DELTA_HEADS, DK, DV), F32)],
        scratch_shapes=[pltpu.VMEM((N_DELTA_HEADS, DK, DV), F32)],
        compiler_params=_cparams("parallel", "arbitrary"),
        name="delta_prompt",
    )(conv_in, conv_in, small, z, w_conv, par, norm)


def _delta_sample_kernel(r0_ref, r1_ref, r2_ref, r3_ref, small_ref, z_ref, s_ref, wconv_ref, par_ref, norm_ref,
                         o_ref, sout_ref):
    w = wconv_ref[...]
    conv = r0_ref[...] * w[0:1] + r1_ref[...] * w[1:2] + r2_ref[...] * w[2:3] + r3_ref[...] * w[3:4]
    conv = _silu(conv)
    g_all, beta_all = _decay_beta(small_ref[...], par_ref)
    sb = conv.shape[0]
    eye = (lax.broadcasted_iota(jnp.int32, (DK, DK), 0) == lax.broadcasted_iota(jnp.int32, (DK, DK), 1)).astype(F32)
    for h in range(N_DELTA_HEADS):
        lo = h * DK
        q = _l2n(conv[:, lo:lo + DK], DK ** -0.5)
        k = _l2n(conv[:, DELTA_W + lo:DELTA_W + lo + DK])
        v = conv[:, 2 * DELTA_W + lo:2 * DELTA_W + lo + DV]
        eg = jnp.exp(g_all[:, h:h + 1])
        beta = beta_all[:, N_DELTA_HEADS + h:N_DELTA_HEADS + h + 1]
        qk = jnp.sum(q * k, axis=-1, keepdims=True)
        k_t = _dot_nt(eye, k, HI)
        rows = []
        for b in range(sb):
            s_b = s_ref[b, h]
            ks = _dot(k[b:b + 1, :], s_b, HI)
            qs = _dot(q[b:b + 1, :], s_b, HI)
            v_new = beta[b:b + 1, :] * (v[b:b + 1, :] - eg[b:b + 1, :] * ks)
            rows.append(eg[b:b + 1, :] * qs + qk[b:b + 1, :] * v_new)
            sout_ref[b, h] = s_b * eg[b:b + 1, :] + k_t[:, b:b + 1] * v_new
        o = jnp.concatenate(rows, axis=0)
        o_ref[:, lo:lo + DV] = _rms(o, norm_ref[...]) * _silu(z_ref[:, lo:lo + DV])


def _delta_sample(rows4, small, z, state, w_conv, par, norm, sb):
    s = small.shape[0]
    tile = lambda i: (i, 0)
    const = lambda i: (0, 0)
    st = pl.BlockSpec((sb, N_DELTA_HEADS, DK, DV), lambda i: (i, 0, 0, 0))
    return pl.pallas_call(
        _delta_sample_kernel,
        grid=(s // sb,),
        in_specs=[pl.BlockSpec((sb, CONV_DIM), tile)] * 4
                 + [pl.BlockSpec((sb, LANES), tile), pl.BlockSpec((sb, DELTA_W), tile), st,
                    pl.BlockSpec((CONV_W, CONV_DIM), const), pl.BlockSpec((SUBLANES, LANES), const),
                    pl.BlockSpec((1, DV), const)],
        out_specs=[pl.BlockSpec((sb, DELTA_W), tile), st],
        out_shape=[jax.ShapeDtypeStruct((s, DELTA_W), F32),
                   jax.ShapeDtypeStruct((s, N_DELTA_HEADS, DK, DV), F32)],
        compiler_params=_cparams("parallel"),
        name="delta_sample",
    )(*rows4, small, z, state, w_conv, par, norm)


def _outproj_kernel(x_ref, oa_ref, od_ref, ga_ref, gb_ref, gt1_ref, sh2_ref, sc2_ref, gpost_ref, gpre_ref,
                    wa_ref, wd_ref, wo_ref, x1_ref, h2_ref):
    att = _dot(oa_ref[...].astype(BF16), wa_ref[...])
    dlt = _dot(od_ref[...].astype(BF16), wd_ref[...])
    merged = jax.nn.sigmoid(ga_ref[...]) * att + jax.nn.sigmoid(gb_ref[...]) * dlt
    mix = _dot(merged.astype(BF16), wo_ref[...])
    x1 = x_ref[...] + gt1_ref[...] * _rms(mix, gpost_ref[...])
    x1_ref[...] = x1
    h2_ref[...] = _rms(x1, gpre_ref[...]) * (1.0 + sc2_ref[...]) + sh2_ref[...]


def _outproj(x, o_att, o_d, ga, gb, mods, mod_rows, tiles_per_group, tm, g_post, g_pre_ffn, wa, wd, wo):
    t = x.shape[0]
    tile = lambda i: (i, 0)
    const = lambda i: (0, 0)
    wide = pl.BlockSpec((tm, D_MODEL), tile)
    half = pl.BlockSpec((tm, ATTN_Q_W), tile)
    return pl.pallas_call(
        _outproj_kernel,
        grid=(t // tm,),
        in_specs=[wide, half, half, wide, wide,
                  _mod_spec(mod_rows, 2, tiles_per_group),
                  _mod_spec(mod_rows, 3, tiles_per_group),
                  _mod_spec(mod_rows, 4, tiles_per_group),
                  pl.BlockSpec((1, D_MODEL), const), pl.BlockSpec((1, D_MODEL), const),
                  pl.BlockSpec((ATTN_Q_W, D_MODEL), const), pl.BlockSpec((DELTA_W, D_MODEL), const),
                  pl.BlockSpec((D_MODEL, D_MODEL), const)],
        out_specs=[wide, wide],
        out_shape=[jax.ShapeDtypeStruct((t, D_MODEL), F32)] * 2,
        compiler_params=_cparams("parallel"),
        name="outproj",
    )(x, o_att, o_d, ga, gb, mods, mods, mods, g_post, g_pre_ffn, wa, wd, wo)


def _top_rows(s, k, payload=None):
    r = s.shape[0]
    rows = lax.broadcasted_iota(jnp.int32, s.shape, 0).astype(F32)
    pick_from = rows if payload is None else payload
    vals, picks = [], []
    for _ in range(k):
        m = jnp.max(s, axis=0, keepdims=True)
        first = jnp.min(jnp.where(s == m, rows, float(r)), axis=0, keepdims=True)
        sel = rows == first
        vals.append(m)
        picks.append(first if payload is None else jnp.sum(jnp.where(sel, pick_from, 0.0), axis=0, keepdims=True))
        s = jnp.where(sel, -jnp.inf, s)
    return jnp.concatenate(vals, axis=0), jnp.concatenate(picks, axis=0)


def _route_kernel(h_ref, wq_ref, keys_ref, eidx_ref, gate_ref):
    q_t = _dot_nt(wq_ref[...], h_ref[...].astype(BF16))
    k0 = keys_ref[0].astype(BF16)
    k1 = keys_ref[1].astype(BF16)
    eidx, gates = [], []
    for h in range(PEER_HEADS):
        lo = h * 2 * PEER_HALF
        s0 = _dot(k0, q_t[lo:lo + PEER_HALF].astype(BF16))
        s1 = _dot(k1, q_t[lo + PEER_HALF:lo + 2 * PEER_HALF].astype(BF16))
        v0, i0 = _top_rows(s0, PEER_TOPK)
        v1, i1 = _top_rows(s1, PEER_TOPK)
        cand = jnp.concatenate([v0[a:a + 1] + v1 for a in range(PEER_TOPK)], axis=0)
        cidx = jnp.concatenate([i0[a:a + 1] * float(N_KEYS) + i1 for a in range(PEER_TOPK)], axis=0)
        best, e = _top_rows(cand, PEER_TOPK, cidx)
        p = jnp.exp(best - best[0:1])
        gates.append(p / jnp.sum(p, axis=0, keepdims=True))
        eidx.append(e)
    eidx_ref[...] = jnp.concatenate(eidx, axis=0).T.astype(jnp.int32)
    gate_ref[...] = jnp.concatenate(gates, axis=0).T


def _route(h2, wq_t, sub_keys, tm):
    t = h2.shape[0]
    return pl.pallas_call(
        _route_kernel,
        grid=(t // tm,),
        in_specs=[pl.BlockSpec((tm, D_MODEL), lambda i: (i, 0)),
                  pl.BlockSpec(wq_t.shape, lambda i: (0, 0)),
                  pl.BlockSpec(sub_keys.shape, lambda i: (0, 0, 0))],
        out_specs=[pl.BlockSpec((tm, N_SEL), lambda i: (i, 0))] * 2,
        out_shape=[jax.ShapeDtypeStruct((t, N_SEL), jnp.int32), jax.ShapeDtypeStruct((t, N_SEL), F32)],
        compiler_params=_cparams("parallel"),
        name="route",
    )(h2, wq_t, sub_keys)


def _gelu(x):
    return 0.5 * x * (1.0 + lax.erf(x * (2.0 ** -0.5)))


def _peer_kernel(idx_ref, gate_ref, h_ref, x_ref, gt_ref, g_ref, uv_ref, y_ref, buf_ref, f_ref, sem_ref, *, tt):
    i = pl.program_id(0)
    n = pl.num_programs(0) - 1
    groups = N_SEL // SUBLANES
    cblocks = D_MODEL // LANES

    @pl.when(i < n)
    def _():
        slot = i % 2
        for t in range(tt):
            for p in range(N_SEL):
                pltpu.make_async_copy(uv_ref.at[idx_ref[t, p]],
                                      buf_ref.at[slot, t * groups + p // SUBLANES, :, p % SUBLANES, :],
                                      sem_ref.at[slot]).start()

    @pl.when(i > 0)
    def _():
        slot = (i - 1) % 2
        pltpu.make_async_copy(buf_ref.at[slot], buf_ref.at[slot], sem_ref.at[slot]).wait()
        eye = (lax.broadcasted_iota(jnp.int32, (N_SEL, N_SEL), 0)
               == lax.broadcasted_iota(jnp.int32, (N_SEL, N_SEL), 1)).astype(F32)
        gate_t = _dot_nt(eye, gate_ref[...], HI)
        lane = lax.broadcasted_iota(jnp.int32, (N_SEL, tt), 1)
        hmat = jnp.zeros((N_SEL, tt), F32)
        for t in range(tt):
            xs = [h_ref[t:t + 1, c * LANES:(c + 1) * LANES] for c in range(cblocks)]
            cols = []
            for g in range(groups):
                acc = buf_ref[slot, t * groups + g, 0] * xs[0]
                for c in range(1, cblocks):
                    acc = acc + buf_ref[slot, t * groups + g, c] * xs[c]
                cols.append(jnp.sum(acc, axis=-1, keepdims=True))
            hmat = jnp.where(lane == t, jnp.concatenate(cols, axis=0), hmat)
        coef = gate_t * _gelu(hmat)
        for t in range(tt):
            cs = [coef[g * SUBLANES:(g + 1) * SUBLANES, t:t + 1] for g in range(groups)]
            for c in range(cblocks):
                acc = cs[0] * buf_ref[slot, t * groups, cblocks + c]
                for g in range(1, groups):
                    acc = acc + cs[g] * buf_ref[slot, t * groups + g, cblocks + c]
                f_ref[t:t + 1, c * LANES:(c + 1) * LANES] = jnp.sum(acc, axis=0, keepdims=True)
        y_ref[...] = x_ref[...] + gt_ref[...] * _rms(f_ref[...], g_ref[...])


def _peer(eidx, gate, h2, x1, mods, mod_rows, tiles_per_group, tt, g_post, uv):
    t = h2.shape[0]
    n = t // tt
    done = lambda i: (jnp.maximum(i - 1, 0), 0)
    return pl.pallas_call(
        functools.partial(_peer_kernel, tt=tt),
        grid=(n + 1,),
        in_specs=[pl.BlockSpec((tt, N_SEL), lambda i: (jnp.minimum(i, n - 1), 0), memory_space=pltpu.SMEM),
                  pl.BlockSpec((tt, N_SEL), done),
                  pl.BlockSpec((tt, D_MODEL), done),
                  pl.BlockSpec((tt, D_MODEL), done),
                  pl.BlockSpec((None, mod_rows, D_MODEL),
                               lambda i: (jnp.maximum(i - 1, 0) // tiles_per_group, 0, N_ADA - 1)),
                  pl.BlockSpec((1, D_MODEL), lambda i: (0, 0)),
                  pl.BlockSpec(memory_space=pl.ANY)],
        out_specs=pl.BlockSpec((tt, D_MODEL), done),
        out_shape=jax.ShapeDtypeStruct((t, D_MODEL), F32),
        scratch_shapes=[pltpu.VMEM((2, tt * N_SEL // SUBLANES, 2 * D_MODEL // LANES, SUBLANES, LANES), F32),
                        pltpu.VMEM((tt, D_MODEL), F32),
                        pltpu.SemaphoreType.DMA((2,))],
        compiler_params=_cparams("arbitrary"),
        name="peer",
    )(eidx, gate, h2, x1, mods, g_post, uv)


def _token_path(x, mods, mod_rows, tiles_per_group, tm, tt, mixer, p):
    attn, conv_in, small, z, ga, gb = _inproj(x, mods, mod_rows, tiles_per_group, tm, p["g_pre_mix"], p["w_in_parts"])
    o_att, o_d, extras = mixer(attn, conv_in, small, z)
    x1, h2 = _outproj(x, o_att, o_d, ga, gb, mods, mod_rows, tiles_per_group, tm, p["g_post_mix"], p["g_pre_ffn"],
                      p["w_att_out"], p["w_delta_out"], p["w_out"])
    eidx, gate = _route(h2, p["wq_t"], p["sub_keys"], tm)
    if mod_rows == 1:
        y = _peer(eidx, gate, h2, x1, mods, 1, tiles_per_group * tm // tt, tt, p["g_post_ffn"], p["uv"])
    else:
        y = _peer(eidx, gate, h2, x1, mods.reshape(-1, tt, N_ADA * D_MODEL), tt, 1, tt, p["g_post_ffn"], p["uv"])
    return y, extras


def _bias_tables(rel_bias):
    qi = jnp.arange(WINDOW, dtype=jnp.int32)[:, None]
    kj = jnp.arange(2 * WINDOW, dtype=jnp.int32)[None, :]
    dist = qi + WINDOW - kj
    valid = (dist >= 0) & (dist < WINDOW)
    bias = jnp.moveaxis(rel_bias[_rel_bucket(dist)], -1, 0)
    later = jnp.where(valid[None], bias, NEG_BIG)
    first = jnp.where((valid & (kj >= WINDOW))[None], bias, NEG_BIG)
    dist_s = WINDOW - 1 - jnp.arange(WINDOW, dtype=jnp.int32)
    bias_s = rel_bias[_rel_bucket(dist_s)].T
    return jnp.stack([first, later]), bias_s


def _layer(xp, xs, c_all, k_buf, v_buf, conv_buf, s0, rel_bias, lp):
    batch, seq, _ = xp.shape
    dec = xs.shape[0]
    assert xs.shape[1] == 1 and seq % WINDOW == 0 and k_buf.shape[1] == WINDOW

    ada = _ada(c_all, lp["w_ada"], lp["b_ada"])
    mods_p = ada[:batch].reshape(batch, 1, N_ADA * D_MODEL)
    mods_s = ada[batch:].reshape(1, dec, N_ADA * D_MODEL)

    w_in = lp["w_in"]
    cuts = (0, ATTN_W, ATTN_W + CONV_DIM, ATTN_W + CONV_DIM + 2 * N_DELTA_HEADS)
    cuts = cuts + (cuts[-1] + DELTA_W, cuts[-1] + DELTA_W + D_MODEL, cuts[-1] + DELTA_W + 2 * D_MODEL)
    parts = [w_in[:, a:b] for a, b in zip(cuts[:-1], cuts[1:])]
    parts[2] = jnp.pad(parts[2], ((0, 0), (0, LANES - 2 * N_DELTA_HEADS)))
    row = lambda v: v.reshape(1, -1)
    par = jnp.zeros((SUBLANES, LANES), F32).at[0, :N_DELTA_HEADS].set(lp["a_log"]).at[1, :N_DELTA_HEADS].set(lp["dt_bias"])
    p = dict(
        w_in_parts=[w.astype(BF16) for w in parts],
        g_pre_mix=row(lp["g_pre_mix"]), g_post_mix=row(lp["g_post_mix"]),
        g_pre_ffn=row(lp["g_pre_ffn"]), g_post_ffn=row(lp["g_post_ffn"]),
        w_att_out=lp["w_att_out"].astype(BF16), w_delta_out=lp["w_delta_out"].astype(BF16),
        w_out=lp["w_out"].astype(BF16),
        wq_t=lp["w_query"].T.astype(BF16), sub_keys=lp["sub_keys"],
        uv=jnp.concatenate([lp["expert_u"], lp["expert_v"]], axis=1).reshape(-1, 2 * D_MODEL // LANES, LANES),
    )
    norm = row(lp["delta_norm"])
    bias2, bias_s = _bias_tables(rel_bias)
    out = {}

    def mixer_prompt(attn, conv_in, small, z):
        o_att = _swa_prompt(attn, batch, seq, lp["attn_sinks"], bias2)
        o_d, s_new = _delta_prompt(conv_in, small, z, batch, seq, lp["w_conv"], par, norm)
        a3 = attn.reshape(batch, seq, ATTN_W)
        new_k = a3[:, -WINDOW:, ATTN_Q_W:ATTN_Q_W + ATTN_KV_W].reshape(batch, WINDOW, N_KV_HEADS, HEAD_DIM)
        new_v = a3[:, -WINDOW:, ATTN_Q_W + ATTN_KV_W:].reshape(batch, WINDOW, N_KV_HEADS, HEAD_DIM)
        new_conv = conv_in.reshape(batch, seq, CONV_DIM)[:, -(CONV_W - 1):]
        return o_att, o_d, (new_k, new_v, new_conv, s_new)

    def mixer_sample(attn, conv_in, small, z):
        k_new = attn[:, ATTN_Q_W:ATTN_Q_W + ATTN_KV_W].reshape(dec, 1, N_KV_HEADS, HEAD_DIM)
        v_new = attn[:, ATTN_Q_W + ATTN_KV_W:].reshape(dec, 1, N_KV_HEADS, HEAD_DIM)
        new_k = jnp.concatenate([k_buf[:, 1:], k_new], axis=1)
        new_v = jnp.concatenate([v_buf[:, 1:], v_new], axis=1)
        sb = SUBLANES if dec % SUBLANES == 0 else dec
        o_att = _swa_sample(attn[:, :ATTN_Q_W].reshape(dec, N_Q_HEADS, HEAD_DIM),
                            new_k.reshape(dec, WINDOW, ATTN_KV_W), new_v.reshape(dec, WINDOW, ATTN_KV_W),
                            lp["attn_sinks"], bias_s, sb).reshape(dec, ATTN_Q_W)
        rows4 = [conv_buf[:, i] for i in range(CONV_W - 1)] + [conv_in]
        o_d, s_new = _delta_sample(rows4, small, z, s0, lp["w_conv"], par, norm, sb)
        new_conv = jnp.concatenate([conv_buf[:, 1:], conv_in[:, None]], axis=1)
        return o_att, o_d, (new_k, new_v, new_conv, s_new)

    tm_p = 256 if seq % 256 == 0 else WINDOW
    yp, ex_p = _token_path(xp.reshape(batch * seq, D_MODEL), mods_p, 1, seq // tm_p, tm_p, SUBLANES, mixer_prompt, p)
    tm_s = dec
    ys, ex_s = _token_path(xs.reshape(dec, D_MODEL), mods_s, dec, 1, tm_s, SUBLANES, mixer_sample, p)
    return yp.reshape(batch, seq, D_MODEL), ys.reshape(dec, 1, D_MODEL), ex_p, ex_s


def kernel(x_prompt, x_sample, cache_swa_k, cache_swa_v, state_conv, state_delta, c_prompt, c_sample, rel_bias, w_ada, b_ada, g_pre_mix, g_post_mix, g_pre_ffn, g_post_ffn, w_in, attn_sinks, w_conv, a_log, dt_bias, delta_norm, w_att_out, w_delta_out, w_out, w_query, sub_keys, expert_u, expert_v):
    depth = w_in.shape[0]
    yp, ys = x_prompt, x_sample
    c_all = jnp.concatenate([c_prompt, c_sample], axis=0)
    ex_p_all, ex_s_all = [], []
    for layer in range(depth):
        lp = dict(w_ada=w_ada[layer], b_ada=b_ada[layer], g_pre_mix=g_pre_mix[layer], g_post_mix=g_post_mix[layer],
                  g_pre_ffn=g_pre_ffn[layer], g_post_ffn=g_post_ffn[layer], w_in=w_in[layer],
                  attn_sinks=attn_sinks[layer], w_conv=w_conv[layer], a_log=a_log[layer], dt_bias=dt_bias[layer],
                  delta_norm=delta_norm[layer], w_att_out=w_att_out[layer], w_delta_out=w_delta_out[layer],
                  w_out=w_out[layer], w_query=w_query[layer], sub_keys=sub_keys[layer],
                  expert_u=expert_u[layer], expert_v=expert_v[layer])
        yp, ys, ex_p, ex_s = _layer(yp, ys, c_all, cache_swa_k[layer], cache_swa_v[layer], state_conv[layer],
                                    state_delta[layer], rel_bias, lp)
        ex_p_all.append(ex_p)
        ex_s_all.append(ex_s)
    stack = lambda exs, j: jnp.stack([e[j] for e in exs])
    return (yp, ys, stack(ex_p_all, 0), stack(ex_p_all, 1), stack(ex_p_all, 2), stack(ex_p_all, 3),
            stack(ex_s_all, 0), stack(ex_s_all, 1), stack(ex_s_all, 2), stack(ex_s_all, 3))
```

```python
import functools
import math

import jax
import jax.numpy as jnp
from jax import lax
from jax.experimental import pallas as pl
from jax.experimental.pallas import tpu as pltpu

F32 = jnp.float32
BF16 = jnp.bfloat16
HI = lax.Precision.HIGHEST

D_MODEL = 1024
N_Q_HEADS = 8
N_KV_HEADS = 2
Q_PER_KV = N_Q_HEADS // N_KV_HEADS
HEAD_DIM = 64
WINDOW = 128
N_BUCKETS = 32
MAX_DISTANCE = 128
N_DELTA_HEADS = 4
DK = 128
DV = 128
CONV_W = 4
CHUNK = 64
N_KEYS = 128
PEER_HEADS = 8
PEER_HALF = 128
PEER_TOPK = 16
RMS_EPS = 1e-6
N_ADA = 6
ATTN_Q_W = N_Q_HEADS * HEAD_DIM
ATTN_KV_W = N_KV_HEADS * HEAD_DIM
ATTN_W = ATTN_Q_W + 2 * ATTN_KV_W
DELTA_W = N_DELTA_HEADS * DK
CONV_DIM = 3 * DELTA_W
N_SEL = PEER_HEADS * PEER_TOPK
LANES = 128
SUBLANES = 8
NEG_BIG = -1e30
VMEM_LIMIT = 56 * 1024 * 1024


def _cparams(*sem):
    return pltpu.CompilerParams(dimension_semantics=sem, vmem_limit_bytes=VMEM_LIMIT)


def _rms(x, g):
    return x * lax.rsqrt(jnp.mean(x * x, axis=-1, keepdims=True) + RMS_EPS) * g


def _silu(x):
    return x * jax.nn.sigmoid(x)


def _dot(a, b, precision=None):
    return jnp.dot(a, b, preferred_element_type=F32, precision=precision)


def _dot_nt(a, b, precision=None):
    return lax.dot_general(a, b, (((1,), (1,)), ((), ())), preferred_element_type=F32, precision=precision)


def _ada_kernel(c_ref, w_ref, b_ref, o_ref):
    o_ref[...] = _dot(_silu(c_ref[...]), w_ref[...], HI) + b_ref[...]


def _ada(c_all, w_ada, b_ada):
    rows = c_all.shape[0]
    n_out = w_ada.shape[1]
    tn = 512
    return pl.pallas_call(
        _ada_kernel,
        grid=(n_out // tn,),
        in_specs=[pl.BlockSpec((rows, D_MODEL), lambda j: (0, 0)),
                  pl.BlockSpec((D_MODEL, tn), lambda j: (0, j)),
                  pl.BlockSpec((1, tn), lambda j: (0, j))],
        out_specs=pl.BlockSpec((rows, tn), lambda j: (0, j)),
        out_shape=jax.ShapeDtypeStruct((rows, n_out), F32),
        compiler_params=_cparams("parallel"),
        name="ada",
    )(c_all, w_ada, b_ada.reshape(1, n_out))


def _mod_spec(rows, chunk, tiles_per_group):
    return pl.BlockSpec((None, rows, D_MODEL), lambda i: (i // tiles_per_group, 0, chunk))


def _inproj_kernel(x_ref, sh_ref, sc_ref, g_ref, wa_ref, wc_ref, ws_ref, wz_ref, wga_ref, wgb_ref,
                   oa_ref, oc_ref, os_ref, oz_ref, oga_ref, ogb_ref):
    h = (_rms(x_ref[...], g_ref[...]) * (1.0 + sc_ref[...]) + sh_ref[...]).astype(BF16)
    for w_ref, o_ref in ((wa_ref, oa_ref), (wc_ref, oc_ref), (ws_ref, os_ref), (wz_ref, oz_ref),
                         (wga_ref, oga_ref), (wgb_ref, ogb_ref)):
        o_ref[...] = _dot(h, w_ref[...])


def _inproj(x, mods, mod_rows, tiles_per_group, tm, g_pre, weights):
    t = x.shape[0]
    widths = [w.shape[1] for w in weights]
    const = lambda i: (0, 0)
    return pl.pallas_call(
        _inproj_kernel,
        grid=(t // tm,),
        in_specs=[pl.BlockSpec((tm, D_MODEL), lambda i: (i, 0)),
                  _mod_spec(mod_rows, 0, tiles_per_group),
                  _mod_spec(mod_rows, 1, tiles_per_group),
                  pl.BlockSpec((1, D_MODEL), const)]
                 + [pl.BlockSpec((D_MODEL, n), const) for n in widths],
        out_specs=[pl.BlockSpec((tm, n), lambda i: (i, 0)) for n in widths],
        out_shape=[jax.ShapeDtypeStruct((t, n), F32) for n in widths],
        compiler_params=_cparams("parallel"),
        name="inproj",
    )(x, mods, mods, g_pre, *weights)


def _rel_bucket(dist):
    n = jnp.maximum(dist, 0)
    max_exact = N_BUCKETS // 2
    nf = jnp.maximum(n, 1).astype(F32)
    large = max_exact + (jnp.log(nf / max_exact) / math.log(MAX_DISTANCE / max_exact)
                         * (N_BUCKETS - max_exact)).astype(jnp.int32)
    return jnp.where(n < max_exact, n, jnp.minimum(large, N_BUCKETS - 1))


def _softmax_sink(s, sink):
    m = jnp.maximum(jnp.max(s, axis=-1, keepdims=True), sink)
    p = jnp.exp(s - m)
    denom = jnp.sum(p, axis=-1, keepdims=True) + jnp.exp(sink - m)
    return p / denom


def _swa_prompt_kernel(sink_ref, q_ref, kp_ref, kc_ref, vp_ref, vc_ref, bias_ref, o_ref):
    q = q_ref[...] * (HEAD_DIM ** -0.5)
    k2 = jnp.concatenate([kp_ref[...], kc_ref[...]], axis=0).astype(BF16)
    v2 = jnp.concatenate([vp_ref[...], vc_ref[...]], axis=0).astype(BF16)
    for h in range(N_Q_HEADS):
        g = h // Q_PER_KV
        qh = q[:, h * HEAD_DIM:(h + 1) * HEAD_DIM].astype(BF16)
        s = _dot_nt(qh, k2[:, g * HEAD_DIM:(g + 1) * HEAD_DIM]) + bias_ref[h]
        p = _softmax_sink(s, sink_ref[h])
        o_ref[:, h * HEAD_DIM:(h + 1) * HEAD_DIM] = _dot(p.astype(BF16), v2[:, g * HEAD_DIM:(g + 1) * HEAD_DIM])


def _swa_prompt(attn, batch, seq, sinks, bias2):
    nb = seq // WINDOW
    qcol = ATTN_Q_W // ATTN_KV_W
    cur = lambda c: (lambda b, j: (b * nb + j, c))
    prev = lambda c: (lambda b, j: (b * nb + jnp.maximum(j - 1, 0), c))
    return pl.pallas_call(
        _swa_prompt_kernel,
        grid=(batch, nb),
        in_specs=[pl.BlockSpec(memory_space=pltpu.SMEM),
                  pl.BlockSpec((WINDOW, ATTN_Q_W), lambda b, j: (b * nb + j, 0)),
                  pl.BlockSpec((WINDOW, ATTN_KV_W), prev(qcol)),
                  pl.BlockSpec((WINDOW, ATTN_KV_W), cur(qcol)),
                  pl.BlockSpec((WINDOW, ATTN_KV_W), prev(qcol + 1)),
                  pl.BlockSpec((WINDOW, ATTN_KV_W), cur(qcol + 1)),
                  pl.BlockSpec((None, N_Q_HEADS, WINDOW, 2 * WINDOW), lambda b, j: (jnp.minimum(j, 1), 0, 0, 0))],
        out_specs=pl.BlockSpec((WINDOW, ATTN_Q_W), lambda b, j: (b * nb + j, 0)),
        out_shape=jax.ShapeDtypeStruct((batch * seq, ATTN_Q_W), F32),
        compiler_params=_cparams("parallel", "parallel"),
        name="swa_prompt",
    )(sinks, attn, attn, attn, attn, attn, bias2)


def _swa_sample_kernel(sink_ref, q_ref, k_ref, v_ref, bias_ref, o_ref):
    q = q_ref[...] * (HEAD_DIM ** -0.5)
    q2 = jnp.concatenate([q, q], axis=-1)
    head = lax.broadcasted_iota(jnp.int32, q2.shape, 1)
    lane = lax.broadcasted_iota(jnp.int32, q2.shape, 2)
    qm = jnp.where(lane // HEAD_DIM == head // Q_PER_KV, q2, 0.0).astype(BF16)
    s = jnp.einsum("shc,sjc->shj", qm, k_ref[...].astype(BF16), preferred_element_type=F32)
    p = _softmax_sink(s + bias_ref[...][None], sink_ref[...][None])
    o2 = jnp.einsum("shj,sjc->shc", p.astype(BF16), v_ref[...].astype(BF16), preferred_element_type=F32)
    head_o = lax.broadcasted_iota(jnp.int32, o_ref.shape, 1)
    o_ref[...] = jnp.where(head_o < Q_PER_KV, o2[:, :, :HEAD_DIM], o2[:, :, HEAD_DIM:])


def _swa_sample(q, k_win, v_win, sinks, bias_s, sb):
    s = q.shape[0]
    return pl.pallas_call(
        _swa_sample_kernel,
        grid=(s // sb,),
        in_specs=[pl.BlockSpec((N_Q_HEADS, 1), lambda i: (0, 0)),
                  pl.BlockSpec((sb, N_Q_HEADS, HEAD_DIM), lambda i: (i, 0, 0)),
                  pl.BlockSpec((sb, WINDOW, ATTN_KV_W), lambda i: (i, 0, 0)),
                  pl.BlockSpec((sb, WINDOW, ATTN_KV_W), lambda i: (i, 0, 0)),
                  pl.BlockSpec((N_Q_HEADS, WINDOW), lambda i: (0, 0))],
        out_specs=pl.BlockSpec((sb, N_Q_HEADS, HEAD_DIM), lambda i: (i, 0, 0)),
        out_shape=jax.ShapeDtypeStruct((s, N_Q_HEADS, HEAD_DIM), F32),
        compiler_params=_cparams("parallel"),
        name="swa_sample",
    )(sinks.reshape(N_Q_HEADS, 1), q, k_win, v_win, bias_s)


def _l2n(x, scale=1.0):
    return x * (lax.rsqrt(jnp.sum(x * x, axis=-1, keepdims=True) + RMS_EPS) * scale)


def _softplus(x):
    return jnp.maximum(x, 0.0) + jnp.log(1.0 + jnp.exp(-jnp.abs(x)))


def _decay_beta(small, par_ref):
    g = -jnp.exp(par_ref[0:1, :]) * _softplus(small + par_ref[1:2, :])
    return g, jax.nn.sigmoid(small)


def _split(x):
    hi = x.astype(BF16)
    return hi, (x - hi.astype(F32)).astype(BF16)


def _dot3(a, b, nt=False):
    mm = _dot_nt if nt else _dot
    return mm(a[0], b[0]) + (mm(a[0], b[1]) + mm(a[1], b[0]))


def _unit_lower_inverse(lm):
    c = lm.shape[0]
    eye = (lax.broadcasted_iota(jnp.int32, (c, c), 0) == lax.broadcasted_iota(jnp.int32, (c, c), 1)).astype(F32)
    power = -lm
    total = eye + power
    k = 2
    while k <= c:
        ps = _split(power)
        if k == 2:
            power = _dot3(ps, ps)
        else:
            total = total + _dot3(ps, _split(total))
            if k < c:
                power = _dot3(ps, ps)
        k *= 2
        yield
    return total


def _delta_head(conv, gc_col, gc_row, beta_all, h, z_ref_b, norm, s_ref_b, o_ref_b, masks):
    c = CHUNK
    causal, strict = masks
    lo = h * DK
    q = _l2n(conv[:, lo:lo + DK], DK ** -0.5)
    k = _l2n(conv[:, DELTA_W + lo:DELTA_W + lo + DK])
    v = conv[:, 2 * DELTA_W + lo:2 * DELTA_W + lo + DV]
    gcol = gc_col[:, h:h + 1]
    grow = gc_row[h:h + 1, :]
    beta = beta_all[:, N_DELTA_HEADS + h:N_DELTA_HEADS + h + 1]
    decay = jnp.where(causal, jnp.exp(jnp.where(causal, gcol - grow, 0.0)), 0.0)
    kb = k * beta
    kk = _dot3(_split(jnp.concatenate([kb, q], axis=0)), _split(k), nt=True)
    yield
    lm = jnp.where(strict, kk[:c] * decay, 0.0)
    a_intra = jnp.where(causal, kk[c:] * decay, 0.0)
    tmat = yield from _unit_lower_inverse(lm)
    uw = _dot3(_split(tmat), _split(jnp.concatenate([v * beta, kb * jnp.exp(gcol)], axis=1)))
    yield
    s_h = s_ref_b[h]
    ws_qs = _dot3(_split(jnp.concatenate([uw[:, DV:], q * jnp.exp(gcol)], axis=0)), _split(s_h))
    yield
    v_new = uw[:, :DV] - ws_qs[:c]
    v_new_s = _split(v_new)
    o = ws_qs[c:] + _dot3(_split(a_intra), v_new_s)
    g_last = gcol[c - 1:c, :]
    s_ref_b[h] = s_h * jnp.exp(g_last) + _dot3(_split((k * jnp.exp(g_last - gcol)).T), v_new_s)
    o_ref_b[:, lo:lo + DV] = _rms(o, norm) * _silu(z_ref_b[:, lo:lo + DV])


def _delta_prompt_kernel(prev_ref, cur_ref, small_ref, z_ref, wconv_ref, par_ref, norm_ref,
                         o_ref, sout_ref, s_ref, *, nb):
    n = pl.program_id(1)
    c = CHUNK

    @pl.when(n == 0)
    def _():
        s_ref[...] = jnp.zeros_like(s_ref)

    ri = lax.broadcasted_iota(jnp.int32, (c, c), 0)
    ci = lax.broadcasted_iota(jnp.int32, (c, c), 1)
    tri_lo = (ri >= ci).astype(F32)
    tri_up = (ri <= ci).astype(F32)
    w = wconv_ref[...]
    chains = []
    for b in range(nb):
        prev = jnp.where(n > 0, prev_ref[b], 0.0)
        full = jnp.concatenate([prev, cur_ref[b]], axis=0)
        base = SUBLANES - (CONV_W - 1)
        conv = full[base:base + c] * w[0:1]
        for i in range(1, CONV_W):
            conv = conv + full[base + i:base + i + c] * w[i:i + 1]
        conv = _silu(conv)
        g_all, beta_all = _decay_beta(small_ref[b], par_ref)
        gc_col = _dot(tri_lo, g_all, HI)
        gc_row = _dot(g_all.T, tri_up, HI)
        chains += [_delta_head(conv, gc_col, gc_row, beta_all, h, z_ref.at[b], norm_ref[...], s_ref.at[b],
                               o_ref.at[b], (ri >= ci, ri > ci)) for h in range(N_DELTA_HEADS)]
    while chains:
        chains = [ch for ch in chains if next(ch, chains) is not chains]

    @pl.when(n == pl.num_programs(1) - 1)
    def _():
        sout_ref[...] = s_ref[...]


def _delta_prompt(conv_in, small, z, batch, seq, w_conv, par, norm, nb):
    nc = seq // CHUNK
    per = CHUNK // SUBLANES
    tile = lambda b, n: (b, n, 0)
    const = lambda b, n: (0, 0)
    return pl.pallas_call(
        functools.partial(_delta_prompt_kernel, nb=nb),
        grid=(batch // nb, nc),
        in_specs=[pl.BlockSpec((nb, SUBLANES, CONV_DIM), lambda b, n: (b, jnp.maximum(n * per - 1, 0), 0)),
                  pl.BlockSpec((nb, CHUNK, CONV_DIM), tile),
                  pl.BlockSpec((nb, CHUNK, LANES), tile),
                  pl.BlockSpec((nb, CHUNK, DELTA_W), tile),
                  pl.BlockSpec((CONV_W, CONV_DIM), const),
                  pl.BlockSpec((SUBLANES, LANES), const),
                  pl.BlockSpec((1, DV), const)],
        out_specs=[pl.BlockSpec((nb, CHUNK, DELTA_W), tile),
                   pl.BlockSpec((nb, N_DELTA_HEADS, DK, DV), lambda b, n: (b, 0, 0, 0))],
        out_shape=[jax.ShapeDtypeStruct((batch, seq, DELTA_W), F32),
                   jax.ShapeDtypeStruct((batch, N_DELTA_HEADS, DK, DV), F32)],
        scratch_shapes=[pltpu.VMEM((nb, N_DELTA_HEADS, DK, DV), F32)],
        compiler_params=_cparams("parallel", "arbitrary"),
        name="delta_prompt",
    )(conv_in, conv_in, small, z, w_conv, par, norm)


def _delta_sample_kernel(r0_ref, r1_ref, r2_ref, r3_ref, small_ref, z_ref, s_ref, wconv_ref, par_ref, norm_ref,
                         o_ref, sout_ref):
    w = wconv_ref[...]
    conv = r0_ref[...] * w[0:1] + r1_ref[...] * w[1:2] + r2_ref[...] * w[2:3] + r3_ref[...] * w[3:4]
    conv = _silu(conv)
    g_all, beta_all = _decay_beta(small_ref[...], par_ref)
    sb = conv.shape[0]
    eye = (lax.broadcasted_iota(jnp.int32, (DK, DK), 0) == lax.broadcasted_iota(jnp.int32, (DK, DK), 1)).astype(F32)
    for h in range(N_DELTA_HEADS):
        lo = h * DK
        q = _l2n(conv[:, lo:lo + DK], DK ** -0.5)
        k = _l2n(conv[:, DELTA_W + lo:DELTA_W + lo + DK])
        v = conv[:, 2 * DELTA_W + lo:2 * DELTA_W + lo + DV]
        eg = jnp.exp(g_all[:, h:h + 1])
        beta = beta_all[:, N_DELTA_HEADS + h:N_DELTA_HEADS + h + 1]
        qk = jnp.sum(q * k, axis=-1, keepdims=True)
        k_t = _dot_nt(eye, k, HI)
        rows = []
        for b in range(sb):
            s_b = s_ref[b, h]
            ks = _dot(k[b:b + 1, :], s_b, HI)
            qs = _dot(q[b:b + 1, :], s_b, HI)
            v_new = beta[b:b + 1, :] * (v[b:b + 1, :] - eg[b:b + 1, :] * ks)
            rows.append(eg[b:b + 1, :] * qs + qk[b:b + 1, :] * v_new)
            sout_ref[b, h] = s_b * eg[b:b + 1, :] + k_t[:, b:b + 1] * v_new
        o = jnp.concatenate(rows, axis=0)
        o_ref[:, lo:lo + DV] = _rms(o, norm_ref[...]) * _silu(z_ref[:, lo:lo + DV])


def _delta_sample(rows4, small, z, state, w_conv, par, norm, sb):
    s = small.shape[0]
    tile = lambda i: (i, 0)
    const = lambda i: (0, 0)
    st = pl.BlockSpec((sb, N_DELTA_HEADS, DK, DV), lambda i: (i, 0, 0, 0))
    return pl.pallas_call(
        _delta_sample_kernel,
        grid=(s // sb,),
        in_specs=[pl.BlockSpec((sb, CONV_DIM), tile)] * 4
                 + [pl.BlockSpec((sb, LANES), tile), pl.BlockSpec((sb, DELTA_W), tile), st,
                    pl.BlockSpec((CONV_W, CONV_DIM), const), pl.BlockSpec((SUBLANES, LANES), const),
                    pl.BlockSpec((1, DV), const)],
        out_specs=[pl.BlockSpec((sb, DELTA_W), tile), st],
        out_shape=[jax.ShapeDtypeStruct((s, DELTA_W), F32),
                   jax.ShapeDtypeStruct((s, N_DELTA_HEADS, DK, DV), F32)],
        compiler_params=_cparams("parallel"),
        name="delta_sample",
    )(*rows4, small, z, state, w_conv, par, norm)


def _outproj_kernel(x_ref, oa_ref, od_ref, ga_ref, gb_ref, gt1_ref, sh2_ref, sc2_ref, gpost_ref, gpre_ref,
                    wa_ref, wd_ref, wo_ref, x1_ref, h2_ref):
    att = _dot(oa_ref[...].astype(BF16), wa_ref[...])
    dlt = _dot(od_ref[...].astype(BF16), wd_ref[...])
    merged = jax.nn.sigmoid(ga_ref[...]) * att + jax.nn.sigmoid(gb_ref[...]) * dlt
    mix = _dot(merged.astype(BF16), wo_ref[...])
    x1 = x_ref[...] + gt1_ref[...] * _rms(mix, gpost_ref[...])
    x1_ref[...] = x1
    h2_ref[...] = _rms(x1, gpre_ref[...]) * (1.0 + sc2_ref[...]) + sh2_ref[...]


def _outproj(x, o_att, o_d, ga, gb, mods, mod_rows, tiles_per_group, tm, g_post, g_pre_ffn, wa, wd, wo):
    t = x.shape[0]
    tile = lambda i: (i, 0)
    const = lambda i: (0, 0)
    wide = pl.BlockSpec((tm, D_MODEL), tile)
    half = pl.BlockSpec((tm, ATTN_Q_W), tile)
    return pl.pallas_call(
        _outproj_kernel,
        grid=(t // tm,),
        in_specs=[wide, half, half, wide, wide,
                  _mod_spec(mod_rows, 2, tiles_per_group),
                  _mod_spec(mod_rows, 3, tiles_per_group),
                  _mod_spec(mod_rows, 4, tiles_per_group),
                  pl.BlockSpec((1, D_MODEL), const), pl.BlockSpec((1, D_MODEL), const),
                  pl.BlockSpec((ATTN_Q_W, D_MODEL), const), pl.BlockSpec((DELTA_W, D_MODEL), const),
                  pl.BlockSpec((D_MODEL, D_MODEL), const)],
        out_specs=[wide, wide],
        out_shape=[jax.ShapeDtypeStruct((t, D_MODEL), F32)] * 2,
        compiler_params=_cparams("parallel"),
        name="outproj",
    )(x, o_att, o_d, ga, gb, mods, mods, mods, g_post, g_pre_ffn, wa, wd, wo)


def _top_rows(s, k, payload=None):
    r = s.shape[0]
    rows = lax.broadcasted_iota(jnp.int32, s.shape, 0).astype(F32)
    pick_from = rows if payload is None else payload
    vals, picks = [], []
    for _ in range(k):
        m = jnp.max(s, axis=0, keepdims=True)
        first = jnp.min(jnp.where(s == m, rows, float(r)), axis=0, keepdims=True)
        sel = rows == first
        vals.append(m)
        picks.append(first if payload is None else jnp.sum(jnp.where(sel, pick_from, 0.0), axis=0, keepdims=True))
        s = jnp.where(sel, -jnp.inf, s)
    return jnp.concatenate(vals, axis=0), jnp.concatenate(picks, axis=0)


def _pair_candidates(v0, i0, v1, i1):
    k = PEER_TOPK
    assert k == 2 * SUBLANES
    rows = [(slice(0, 1), slice(0, k))]
    rows += [(slice(a, a + 1), slice(0, SUBLANES)) for a in range(1, SUBLANES)]
    rows += [(slice(SUBLANES, k), slice(0, 1))]
    cand = jnp.concatenate([v0[a] + v1[b] for a, b in rows], axis=0)
    cidx = jnp.concatenate([i0[a] * float(N_KEYS) + i1[b] for a, b in rows], axis=0)
    return cand, cidx


def _route_kernel(h_ref, wq_ref, keys_ref, eidx_ref, gate_ref):
    q_t = _dot_nt(wq_ref[...], h_ref[...].astype(BF16))
    k0 = keys_ref[0].astype(BF16)
    k1 = keys_ref[1].astype(BF16)
    eidx, gates = [], []
    for h in range(PEER_HEADS):
        lo = h * 2 * PEER_HALF
        s0 = _dot(k0, q_t[lo:lo + PEER_HALF].astype(BF16))
        s1 = _dot(k1, q_t[lo + PEER_HALF:lo + 2 * PEER_HALF].astype(BF16))
        v0, i0 = _top_rows(s0, PEER_TOPK)
        v1, i1 = _top_rows(s1, PEER_TOPK)
        cand, cidx = _pair_candidates(v0, i0, v1, i1)
        best, e = _top_rows(cand, PEER_TOPK, cidx)
        p = jnp.exp(best - best[0:1])
        gates.append(p / jnp.sum(p, axis=0, keepdims=True))
        eidx.append(e)
    eidx_ref[...] = jnp.concatenate(eidx, axis=0).T.astype(jnp.int32)
    gate_ref[...] = jnp.concatenate(gates, axis=0).T


def _route(h2, wq_t, sub_keys, tm):
    t = h2.shape[0]
    return pl.pallas_call(
        _route_kernel,
        grid=(t // tm,),
        in_specs=[pl.BlockSpec((tm, D_MODEL), lambda i: (i, 0)),
                  pl.BlockSpec(wq_t.shape, lambda i: (0, 0)),
                  pl.BlockSpec(sub_keys.shape, lambda i: (0, 0, 0))],
        out_specs=[pl.BlockSpec((tm, N_SEL), lambda i: (i, 0))] * 2,
        out_shape=[jax.ShapeDtypeStruct((t, N_SEL), jnp.int32), jax.ShapeDtypeStruct((t, N_SEL), F32)],
        compiler_params=_cparams("parallel"),
        name="route",
    )(h2, wq_t, sub_keys)


def _gelu(x):
    return 0.5 * x * (1.0 + lax.erf(x * (2.0 ** -0.5)))


def _peer_kernel(idx_ref, gate_ref, h_ref, x_ref, gt_ref, g_ref, uv_ref, y_ref, buf_ref, f_ref, sem_ref, *, tt):
    i = pl.program_id(0)
    n = pl.num_programs(0) - 1
    groups = N_SEL // SUBLANES
    cblocks = D_MODEL // LANES

    @pl.when(i < n)
    def _():
        slot = i % 2
        for t in range(tt):
            for p in range(N_SEL):
                pltpu.make_async_copy(uv_ref.at[idx_ref[t, p]],
                                      buf_ref.at[slot, t * groups + p // SUBLANES, :, p % SUBLANES, :],
                                      sem_ref.at[slot]).start(priority=p % 2)

    @pl.when(i > 0)
    def _():
        slot = (i - 1) % 2
        pltpu.make_async_copy(buf_ref.at[slot], buf_ref.at[slot], sem_ref.at[slot]).wait()
        eye = (lax.broadcasted_iota(jnp.int32, (N_SEL, N_SEL), 0)
               == lax.broadcasted_iota(jnp.int32, (N_SEL, N_SEL), 1)).astype(F32)
        gate_t = _dot_nt(eye, gate_ref[...], HI)
        lane = lax.broadcasted_iota(jnp.int32, (N_SEL, tt), 1)
        hmat = jnp.zeros((N_SEL, tt), F32)
        for t in range(tt):
            xs = [h_ref[t:t + 1, c * LANES:(c + 1) * LANES] for c in range(cblocks)]
            cols = []
            for g in range(groups):
                acc = buf_ref[slot, t * groups + g, 0] * xs[0]
                for c in range(1, cblocks):
                    acc = acc + buf_ref[slot, t * groups + g, c] * xs[c]
                cols.append(jnp.sum(acc, axis=-1, keepdims=True))
            hmat = jnp.where(lane == t, jnp.concatenate(cols, axis=0), hmat)
        coef = gate_t * _gelu(hmat)
        for t in range(tt):
            cs = [coef[g * SUBLANES:(g + 1) * SUBLANES, t:t + 1] for g in range(groups)]
            for c in range(cblocks):
                acc = cs[0] * buf_ref[slot, t * groups, cblocks + c]
                for g in range(1, groups):
                    acc = acc + cs[g] * buf_ref[slot, t * groups + g, cblocks + c]
                f_ref[t:t + 1, c * LANES:(c + 1) * LANES] = jnp.sum(acc, axis=0, keepdims=True)
        y_ref[...] = x_ref[...] + gt_ref[...] * _rms(f_ref[...], g_ref[...])


def _peer(eidx, gate, h2, x1, mods, mod_rows, tiles_per_group, tt, g_post, uv):
    t = h2.shape[0]
    n = t // tt
    done = lambda i: (jnp.maximum(i - 1, 0), 0)
    return pl.pallas_call(
        functools.partial(_peer_kernel, tt=tt),
        grid=(n + 1,),
        in_specs=[pl.BlockSpec((tt, N_SEL), lambda i: (jnp.minimum(i, n - 1), 0), memory_space=pltpu.SMEM),
                  pl.BlockSpec((tt, N_SEL), done),
                  pl.BlockSpec((tt, D_MODEL), done),
                  pl.BlockSpec((tt, D_MODEL), done),
                  pl.BlockSpec((None, mod_rows, D_MODEL),
                               lambda i: (jnp.maximum(i - 1, 0) // tiles_per_group, 0, N_ADA - 1)),
                  pl.BlockSpec((1, D_MODEL), lambda i: (0, 0)),
                  pl.BlockSpec(memory_space=pl.ANY)],
        out_specs=pl.BlockSpec((tt, D_MODEL), done),
        out_shape=jax.ShapeDtypeStruct((t, D_MODEL), F32),
        scratch_shapes=[pltpu.VMEM((2, tt * N_SEL // SUBLANES, 2 * D_MODEL // LANES, SUBLANES, LANES), F32),
                        pltpu.VMEM((tt, D_MODEL), F32),
                        pltpu.SemaphoreType.DMA((2,))],
        compiler_params=_cparams("arbitrary"),
        name="peer",
    )(eidx, gate, h2, x1, mods, g_post, uv)


def _token_path(x, mods, mod_rows, tiles_per_group, tm, tt, mixer, p):
    attn, conv_in, small, z, ga, gb = _inproj(x, mods, mod_rows, tiles_per_group, tm, p["g_pre_mix"], p["w_in_parts"])
    o_att, o_d, extras = mixer(attn, conv_in, small, z)
    x1, h2 = _outproj(x, o_att, o_d, ga, gb, mods, mod_rows, tiles_per_group, tm, p["g_post_mix"], p["g_pre_ffn"],
                      p["w_att_out"], p["w_delta_out"], p["w_out"])
    eidx, gate = _route(h2, p["wq_t"], p["sub_keys"], tm)
    if mod_rows == 1:
        y = _peer(eidx, gate, h2, x1, mods, 1, tiles_per_group * tm // tt, tt, p["g_post_ffn"], p["uv"])
    else:
        y = _peer(eidx, gate, h2, x1, mods.reshape(-1, tt, N_ADA * D_MODEL), tt, 1, tt, p["g_post_ffn"], p["uv"])
    return y, extras


def _bias_tables(rel_bias):
    qi = jnp.arange(WINDOW, dtype=jnp.int32)[:, None]
    kj = jnp.arange(2 * WINDOW, dtype=jnp.int32)[None, :]
    dist = qi + WINDOW - kj
    valid = (dist >= 0) & (dist < WINDOW)
    bias = jnp.moveaxis(rel_bias[_rel_bucket(dist)], -1, 0)
    later = jnp.where(valid[None], bias, NEG_BIG)
    first = jnp.where((valid & (kj >= WINDOW))[None], bias, NEG_BIG)
    dist_s = WINDOW - 1 - jnp.arange(WINDOW, dtype=jnp.int32)
    bias_s = rel_bias[_rel_bucket(dist_s)].T
    return jnp.stack([first, later]), bias_s


def _layer(xp, xs, c_all, k_buf, v_buf, conv_buf, s0, rel_bias, lp):
    batch, seq, _ = xp.shape
    dec = xs.shape[0]
    assert xs.shape[1] == 1 and seq % WINDOW == 0 and k_buf.shape[1] == WINDOW

    ada = _ada(c_all, lp["w_ada"], lp["b_ada"])
    mods_p = ada[:batch].reshape(batch, 1, N_ADA * D_MODEL)
    mods_s = ada[batch:].reshape(1, dec, N_ADA * D_MODEL)

    w_in = lp["w_in"]
    cuts = (0, ATTN_W, ATTN_W + CONV_DIM, ATTN_W + CONV_DIM + 2 * N_DELTA_HEADS)
    cuts = cuts + (cuts[-1] + DELTA_W, cuts[-1] + DELTA_W + D_MODEL, cuts[-1] + DELTA_W + 2 * D_MODEL)
    parts = [w_in[:, a:b] for a, b in zip(cuts[:-1], cuts[1:])]
    parts[2] = jnp.pad(parts[2], ((0, 0), (0, LANES - 2 * N_DELTA_HEADS)))
    row = lambda v: v.reshape(1, -1)
    par = jnp.zeros((SUBLANES, LANES), F32).at[0, :N_DELTA_HEADS].set(lp["a_log"]).at[1, :N_DELTA_HEADS].set(lp["dt_bias"])
    p = dict(
        w_in_parts=[w.astype(BF16) for w in parts],
        g_pre_mix=row(lp["g_pre_mix"]), g_post_mix=row(lp["g_post_mix"]),
        g_pre_ffn=row(lp["g_pre_ffn"]), g_post_ffn=row(lp["g_post_ffn"]),
        w_att_out=lp["w_att_out"].astype(BF16), w_delta_out=lp["w_delta_out"].astype(BF16),
        w_out=lp["w_out"].astype(BF16),
        wq_t=lp["w_query"].T.astype(BF16), sub_keys=lp["sub_keys"],
        uv=jnp.concatenate([lp["expert_u"], lp["expert_v"]], axis=1).reshape(-1, 2 * D_MODEL // LANES, LANES),
    )
    norm = row(lp["delta_norm"])
    bias2, bias_s = _bias_tables(rel_bias)
    out = {}

    def mixer_prompt(attn, conv_in, small, z):
        o_att = _swa_prompt(attn, batch, seq, lp["attn_sinks"], bias2)
        nb = 2 if batch % 2 == 0 else 1
        o_d, s_new = _delta_prompt(conv_in.reshape(batch, seq, CONV_DIM), small.reshape(batch, seq, LANES),
                                   z.reshape(batch, seq, DELTA_W), batch, seq, lp["w_conv"], par, norm, nb)
        o_d = o_d.reshape(batch * seq, DELTA_W)
        a3 = attn.reshape(batch, seq, ATTN_W)
        new_k = a3[:, -WINDOW:, ATTN_Q_W:ATTN_Q_W + ATTN_KV_W].reshape(batch, WINDOW, N_KV_HEADS, HEAD_DIM)
        new_v = a3[:, -WINDOW:, ATTN_Q_W + ATTN_KV_W:].reshape(batch, WINDOW, N_KV_HEADS, HEAD_DIM)
        new_conv = conv_in.reshape(batch, seq, CONV_DIM)[:, -(CONV_W - 1):]
        return o_att, o_d, (new_k, new_v, new_conv, s_new)

    def mixer_sample(attn, conv_in, small, z):
        k_new = attn[:, ATTN_Q_W:ATTN_Q_W + ATTN_KV_W].reshape(dec, 1, N_KV_HEADS, HEAD_DIM)
        v_new = attn[:, ATTN_Q_W + ATTN_KV_W:].reshape(dec, 1, N_KV_HEADS, HEAD_DIM)
        new_k = jnp.concatenate([k_buf[:, 1:], k_new], axis=1)
        new_v = jnp.concatenate([v_buf[:, 1:], v_new], axis=1)
        sb = SUBLANES if dec % SUBLANES == 0 else dec
        o_att = _swa_sample(attn[:, :ATTN_Q_W].reshape(dec, N_Q_HEADS, HEAD_DIM),
                            new_k.reshape(dec, WINDOW, ATTN_KV_W), new_v.reshape(dec, WINDOW, ATTN_KV_W),
                            lp["attn_sinks"], bias_s, sb).reshape(dec, ATTN_Q_W)
        rows4 = [conv_buf[:, i] for i in range(CONV_W - 1)] + [conv_in]
        o_d, s_new = _delta_sample(rows4, small, z, s0, lp["w_conv"], par, norm, sb)
        new_conv = jnp.concatenate([conv_buf[:, 1:], conv_in[:, None]], axis=1)
        return o_att, o_d, (new_k, new_v, new_conv, s_new)

    tm_p = 256 if seq % 256 == 0 else WINDOW
    yp, ex_p = _token_path(xp.reshape(batch * seq, D_MODEL), mods_p, 1, seq // tm_p, tm_p, SUBLANES, mixer_prompt, p)
    tm_s = dec
    ys, ex_s = _token_path(xs.reshape(dec, D_MODEL), mods_s, dec, 1, tm_s, SUBLANES, mixer_sample, p)
    return yp.reshape(batch, seq, D_MODEL), ys.reshape(dec, 1, D_MODEL), ex_p, ex_s


def kernel(x_prompt, x_sample, cache_swa_k, cache_swa_v, state_conv, state_delta, c_prompt, c_sample, rel_bias, w_ada, b_ada, g_pre_mix, g_post_mix, g_pre_ffn, g_post_ffn, w_in, attn_sinks, w_conv, a_log, dt_bias, delta_norm, w_att_out, w_delta_out, w_out, w_query, sub_keys, expert_u, expert_v):
    depth = w_in.shape[0]
    yp, ys = x_prompt, x_sample
    c_all = jnp.concatenate([c_prompt, c_sample], axis=0)
    ex_p_all, ex_s_all = [], []
    for layer in range(depth):
        lp = dict(w_ada=w_ada[layer], b_ada=b_ada[layer], g_pre_mix=g_pre_mix[layer], g_post_mix=g_post_mix[layer],
                  g_pre_ffn=g_pre_ffn[layer], g_post_ffn=g_post_ffn[layer], w_in=w_in[layer],
                  attn_sinks=attn_sinks[layer], w_conv=w_conv[layer], a_log=a_log[layer], dt_bias=dt_bias[layer],
                  delta_norm=delta_norm[layer], w_att_out=w_att_out[layer], w_delta_out=w_delta_out[layer],
                  w_out=w_out[layer], w_query=w_query[layer], sub_keys=sub_keys[layer],
                  expert_u=expert_u[layer], expert_v=expert_v[layer])
        yp, ys, ex_p, ex_s = _layer(yp, ys, c_all, cache_swa_k[layer], cache_swa_v[layer], state_conv[layer],
                                    state_delta[layer], rel_bias, lp)
        ex_p_all.append(ex_p)
        ex_s_all.append(ex_s)
    stack = lambda exs, j: jnp.stack([e[j] for e in exs])
    return (yp, ys, stack(ex_p_all, 0), stack(ex_p_all, 1), stack(ex_p_all, 2), stack(ex_p_all, 3),
            stack(ex_s_all, 0), stack(ex_s_all, 1), stack(ex_s_all, 2), stack(ex_s_all, 3))
```

```python
import functools
import math

import jax
import jax.numpy as jnp
from jax import lax
from jax.experimental import pallas as pl
from jax.experimental.pallas import tpu as pltpu

F32 = jnp.float32
BF16 = jnp.bfloat16
HI = lax.Precision.HIGHEST

D_MODEL = 1024
N_Q_HEADS = 8
N_KV_HEADS = 2
Q_PER_KV = N_Q_HEADS // N_KV_HEADS
HEAD_DIM = 64
WINDOW = 128
N_BUCKETS = 32
MAX_DISTANCE = 128
N_DELTA_HEADS = 4
DK = 128
DV = 128
CONV_W = 4
CHUNK = 64
N_KEYS = 128
PEER_HEADS = 8
PEER_HALF = 128
PEER_TOPK = 16
RMS_EPS = 1e-6
N_ADA = 6
ATTN_Q_W = N_Q_HEADS * HEAD_DIM
ATTN_KV_W = N_KV_HEADS * HEAD_DIM
ATTN_W = ATTN_Q_W + 2 * ATTN_KV_W
DELTA_W = N_DELTA_HEADS * DK
CONV_DIM = 3 * DELTA_W
N_SEL = PEER_HEADS * PEER_TOPK
LANES = 128
SUBLANES = 8
NEG_BIG = -1e30
VMEM_LIMIT = 56 * 1024 * 1024


def _cparams(*sem):
    return pltpu.CompilerParams(dimension_semantics=sem, vmem_limit_bytes=VMEM_LIMIT)


def _rms(x, g):
    return x * lax.rsqrt(jnp.mean(x * x, axis=-1, keepdims=True) + RMS_EPS) * g


def _silu(x):
    return x * jax.nn.sigmoid(x)


def _dot(a, b, precision=None):
    return jnp.dot(a, b, preferred_element_type=F32, precision=precision)


def _dot_nt(a, b, precision=None):
    return lax.dot_general(a, b, (((1,), (1,)), ((), ())), preferred_element_type=F32, precision=precision)


def _ada_kernel(c_ref, w_ref, b_ref, o_ref):
    o_ref[...] = _dot(_silu(c_ref[...]), w_ref[...], HI) + b_ref[...]


def _ada(c_all, w_ada, b_ada):
    rows = c_all.shape[0]
    n_out = w_ada.shape[1]
    tn = 512
    return pl.pallas_call(
        _ada_kernel,
        grid=(n_out // tn,),
        in_specs=[pl.BlockSpec((rows, D_MODEL), lambda j: (0, 0)),
                  pl.BlockSpec((D_MODEL, tn), lambda j: (0, j)),
                  pl.BlockSpec((1, tn), lambda j: (0, j))],
        out_specs=pl.BlockSpec((rows, tn), lambda j: (0, j)),
        out_shape=jax.ShapeDtypeStruct((rows, n_out), F32),
        compiler_params=_cparams("parallel"),
        name="ada",
    )(c_all, w_ada, b_ada.reshape(1, n_out))


def _mod_spec(rows, chunk, tiles_per_group):
    return pl.BlockSpec((None, rows, D_MODEL), lambda i: (i // tiles_per_group, 0, chunk))


def _inproj_kernel(x_ref, sh_ref, sc_ref, g_ref, wa_ref, wc_ref, ws_ref, wz_ref, wga_ref, wgb_ref,
                   oa_ref, oc_ref, os_ref, oz_ref, oga_ref, ogb_ref):
    h = (_rms(x_ref[...], g_ref[...]) * (1.0 + sc_ref[...]) + sh_ref[...]).astype(BF16)
    for w_ref, o_ref in ((wa_ref, oa_ref), (wc_ref, oc_ref), (ws_ref, os_ref), (wz_ref, oz_ref),
                         (wga_ref, oga_ref), (wgb_ref, ogb_ref)):
        o_ref[...] = _dot(h, w_ref[...])


def _inproj(x, mods, mod_rows, tiles_per_group, tm, g_pre, weights):
    t = x.shape[0]
    widths = [w.shape[1] for w in weights]
    const = lambda i: (0, 0)
    return pl.pallas_call(
        _inproj_kernel,
        grid=(t // tm,),
        in_specs=[pl.BlockSpec((tm, D_MODEL), lambda i: (i, 0)),
                  _mod_spec(mod_rows, 0, tiles_per_group),
                  _mod_spec(mod_rows, 1, tiles_per_group),
                  pl.BlockSpec((1, D_MODEL), const)]
                 + [pl.BlockSpec((D_MODEL, n), const) for n in widths],
        out_specs=[pl.BlockSpec((tm, n), lambda i: (i, 0)) for n in widths],
        out_shape=[jax.ShapeDtypeStruct((t, n), F32) for n in widths],
        compiler_params=_cparams("parallel"),
        name="inproj",
    )(x, mods, mods, g_pre, *weights)


def _rel_bucket(dist):
    n = jnp.maximum(dist, 0)
    max_exact = N_BUCKETS // 2
    nf = jnp.maximum(n, 1).astype(F32)
    large = max_exact + (jnp.log(nf / max_exact) / math.log(MAX_DISTANCE / max_exact)
                         * (N_BUCKETS - max_exact)).astype(jnp.int32)
    return jnp.where(n < max_exact, n, jnp.minimum(large, N_BUCKETS - 1))


def _softmax_sink(s, sink):
    m = jnp.maximum(jnp.max(s, axis=-1, keepdims=True), sink)
    p = jnp.exp(s - m)
    denom = jnp.sum(p, axis=-1, keepdims=True) + jnp.exp(sink - m)
    return p / denom


def _swa_prompt_kernel(sink_ref, q_ref, kp_ref, kc_ref, vp_ref, vc_ref, bias_ref, o_ref):
    q = q_ref[...] * (HEAD_DIM ** -0.5)
    k2 = jnp.concatenate([kp_ref[...], kc_ref[...]], axis=0).astype(BF16)
    v2 = jnp.concatenate([vp_ref[...], vc_ref[...]], axis=0).astype(BF16)
    for h in range(N_Q_HEADS):
        g = h // Q_PER_KV
        qh = q[:, h * HEAD_DIM:(h + 1) * HEAD_DIM].astype(BF16)
        s = _dot_nt(qh, k2[:, g * HEAD_DIM:(g + 1) * HEAD_DIM]) + bias_ref[h]
        p = _softmax_sink(s, sink_ref[h])
        o_ref[:, h * HEAD_DIM:(h + 1) * HEAD_DIM] = _dot(p.astype(BF16), v2[:, g * HEAD_DIM:(g + 1) * HEAD_DIM])


def _swa_prompt(attn, batch, seq, sinks, bias2):
    nb = seq // WINDOW
    qcol = ATTN_Q_W // ATTN_KV_W
    cur = lambda c: (lambda b, j: (b * nb + j, c))
    prev = lambda c: (lambda b, j: (b * nb + jnp.maximum(j - 1, 0), c))
    return pl.pallas_call(
        _swa_prompt_kernel,
        grid=(batch, nb),
        in_specs=[pl.BlockSpec(memory_space=pltpu.SMEM),
                  pl.BlockSpec((WINDOW, ATTN_Q_W), lambda b, j: (b * nb + j, 0)),
                  pl.BlockSpec((WINDOW, ATTN_KV_W), prev(qcol)),
                  pl.BlockSpec((WINDOW, ATTN_KV_W), cur(qcol)),
                  pl.BlockSpec((WINDOW, ATTN_KV_W), prev(qcol + 1)),
                  pl.BlockSpec((WINDOW, ATTN_KV_W), cur(qcol + 1)),
                  pl.BlockSpec((None, N_Q_HEADS, WINDOW, 2 * WINDOW), lambda b, j: (jnp.minimum(j, 1), 0, 0, 0))],
        out_specs=pl.BlockSpec((WINDOW, ATTN_Q_W), lambda b, j: (b * nb + j, 0)),
        out_shape=jax.ShapeDtypeStruct((batch * seq, ATTN_Q_W), F32),
        compiler_params=_cparams("parallel", "parallel"),
        name="swa_prompt",
    )(sinks, attn, attn, attn, attn, attn, bias2)


def _swa_sample_kernel(sink_ref, q_ref, k_ref, v_ref, bias_ref, o_ref):
    q = q_ref[...] * (HEAD_DIM ** -0.5)
    q2 = jnp.concatenate([q, q], axis=-1)
    head = lax.broadcasted_iota(jnp.int32, q2.shape, 1)
    lane = lax.broadcasted_iota(jnp.int32, q2.shape, 2)
    qm = jnp.where(lane // HEAD_DIM == head // Q_PER_KV, q2, 0.0).astype(BF16)
    s = jnp.einsum("shc,sjc->shj", qm, k_ref[...].astype(BF16), preferred_element_type=F32)
    p = _softmax_sink(s + bias_ref[...][None], sink_ref[...][None])
    o2 = jnp.einsum("shj,sjc->shc", p.astype(BF16), v_ref[...].astype(BF16), preferred_element_type=F32)
    head_o = lax.broadcasted_iota(jnp.int32, o_ref.shape, 1)
    o_ref[...] = jnp.where(head_o < Q_PER_KV, o2[:, :, :HEAD_DIM], o2[:, :, HEAD_DIM:])


def _swa_sample(q, k_win, v_win, sinks, bias_s, sb):
    s = q.shape[0]
    return pl.pallas_call(
        _swa_sample_kernel,
        grid=(s // sb,),
        in_specs=[pl.BlockSpec((N_Q_HEADS, 1), lambda i: (0, 0)),
                  pl.BlockSpec((sb, N_Q_HEADS, HEAD_DIM), lambda i: (i, 0, 0)),
                  pl.BlockSpec((sb, WINDOW, ATTN_KV_W), lambda i: (i, 0, 0)),
                  pl.BlockSpec((sb, WINDOW, ATTN_KV_W), lambda i: (i, 0, 0)),
                  pl.BlockSpec((N_Q_HEADS, WINDOW), lambda i: (0, 0))],
        out_specs=pl.BlockSpec((sb, N_Q_HEADS, HEAD_DIM), lambda i: (i, 0, 0)),
        out_shape=jax.ShapeDtypeStruct((s, N_Q_HEADS, HEAD_DIM), F32),
        compiler_params=_cparams("parallel"),
        name="swa_sample",
    )(sinks.reshape(N_Q_HEADS, 1), q, k_win, v_win, bias_s)


def _l2n(x, scale=1.0):
    return x * (lax.rsqrt(jnp.sum(x * x, axis=-1, keepdims=True) + RMS_EPS) * scale)


def _softplus(x):
    return jnp.maximum(x, 0.0) + jnp.log(1.0 + jnp.exp(-jnp.abs(x)))


def _decay_beta(small, par_ref):
    g = -jnp.exp(par_ref[0:1, :]) * _softplus(small + par_ref[1:2, :])
    return g, jax.nn.sigmoid(small)


def _split(x):
    hi = x.astype(BF16)
    return hi, (x - hi.astype(F32)).astype(BF16)


def _dot3(a, b, nt=False):
    mm = _dot_nt if nt else _dot
    return mm(a[0], b[0]) + (mm(a[0], b[1]) + mm(a[1], b[0]))


def _unit_lower_inverse(lm):
    c = lm.shape[0]
    eye = (lax.broadcasted_iota(jnp.int32, (c, c), 0) == lax.broadcasted_iota(jnp.int32, (c, c), 1)).astype(F32)
    power = -lm
    total = eye + power
    k = 2
    while k <= c:
        ps = _split(power)
        if k == 2:
            power = _dot3(ps, ps)
        else:
            total = total + _dot3(ps, _split(total))
            if k < c:
                power = _dot3(ps, ps)
        k *= 2
        yield
    return total


def _delta_head(conv, gc_col, gc_row, beta_all, h, z_ref_b, norm, s_ref_b, o_ref_b, masks):
    c = CHUNK
    causal, strict = masks
    lo = h * DK
    q = _l2n(conv[:, lo:lo + DK], DK ** -0.5)
    k = _l2n(conv[:, DELTA_W + lo:DELTA_W + lo + DK])
    v = conv[:, 2 * DELTA_W + lo:2 * DELTA_W + lo + DV]
    gcol = gc_col[:, h:h + 1]
    grow = gc_row[h:h + 1, :]
    beta = beta_all[:, N_DELTA_HEADS + h:N_DELTA_HEADS + h + 1]
    decay = jnp.where(causal, jnp.exp(jnp.where(causal, gcol - grow, 0.0)), 0.0)
    kb = k * beta
    kk = _dot3(_split(jnp.concatenate([kb, q], axis=0)), _split(k), nt=True)
    yield
    lm = jnp.where(strict, kk[:c] * decay, 0.0)
    a_intra = jnp.where(causal, kk[c:] * decay, 0.0)
    tmat = yield from _unit_lower_inverse(lm)
    uw = _dot3(_split(tmat), _split(jnp.concatenate([v * beta, kb * jnp.exp(gcol)], axis=1)))
    yield
    s_h = s_ref_b[h]
    ws_qs = _dot3(_split(jnp.concatenate([uw[:, DV:], q * jnp.exp(gcol)], axis=0)), _split(s_h))
    yield
    v_new = uw[:, :DV] - ws_qs[:c]
    v_new_s = _split(v_new)
    o = ws_qs[c:] + _dot3(_split(a_intra), v_new_s)
    g_last = gcol[c - 1:c, :]
    s_ref_b[h] = s_h * jnp.exp(g_last) + _dot3(_split((k * jnp.exp(g_last - gcol)).T), v_new_s)
    o_ref_b[:, lo:lo + DV] = _rms(o, norm) * _silu(z_ref_b[:, lo:lo + DV])


def _delta_prompt_kernel(prev_ref, cur_ref, small_ref, z_ref, wconv_ref, par_ref, norm_ref,
                         o_ref, sout_ref, s_ref, *, nb):
    n = pl.program_id(1)
    c = CHUNK

    @pl.when(n == 0)
    def _():
        s_ref[...] = jnp.zeros_like(s_ref)

    ri = lax.broadcasted_iota(jnp.int32, (c, c), 0)
    ci = lax.broadcasted_iota(jnp.int32, (c, c), 1)
    tri_lo = (ri >= ci).astype(F32)
    tri_up = (ri <= ci).astype(F32)
    w = wconv_ref[...]
    chains = []
    for b in range(nb):
        prev = jnp.where(n > 0, prev_ref[b], 0.0)
        full = jnp.concatenate([prev, cur_ref[b]], axis=0)
        base = SUBLANES - (CONV_W - 1)
        conv = full[base:base + c] * w[0:1]
        for i in range(1, CONV_W):
            conv = conv + full[base + i:base + i + c] * w[i:i + 1]
        conv = _silu(conv)
        g_all, beta_all = _decay_beta(small_ref[b], par_ref)
        gc_col = _dot(tri_lo, g_all, HI)
        gc_row = _dot(g_all.T, tri_up, HI)
        chains += [_delta_head(conv, gc_col, gc_row, beta_all, h, z_ref.at[b], norm_ref[...], s_ref.at[b],
                               o_ref.at[b], (ri >= ci, ri > ci)) for h in range(N_DELTA_HEADS)]
    while chains:
        chains = [ch for ch in chains if next(ch, chains) is not chains]

    @pl.when(n == pl.num_programs(1) - 1)
    def _():
        sout_ref[...] = s_ref[...]


def _delta_prompt(conv_in, small, z, batch, seq, w_conv, par, norm, nb):
    nc = seq // CHUNK
    per = CHUNK // SUBLANES
    tile = lambda b, n: (b, n, 0)
    const = lambda b, n: (0, 0)
    return pl.pallas_call(
        functools.partial(_delta_prompt_kernel, nb=nb),
        grid=(batch // nb, nc),
        in_specs=[pl.BlockSpec((nb, SUBLANES, CONV_DIM), lambda b, n: (b, jnp.maximum(n * per - 1, 0), 0)),
                  pl.BlockSpec((nb, CHUNK, CONV_DIM), tile),
                  pl.BlockSpec((nb, CHUNK, LANES), tile),
                  pl.BlockSpec((nb, CHUNK, DELTA_W), tile),
                  pl.BlockSpec((CONV_W, CONV_DIM), const),
                  pl.BlockSpec((SUBLANES, LANES), const),
                  pl.BlockSpec((1, DV), const)],
        out_specs=[pl.BlockSpec((nb, CHUNK, DELTA_W), tile),
                   pl.BlockSpec((nb, N_DELTA_HEADS, DK, DV), lambda b, n: (b, 0, 0, 0))],
        out_shape=[jax.ShapeDtypeStruct((batch, seq, DELTA_W), F32),
                   jax.ShapeDtypeStruct((batch, N_DELTA_HEADS, DK, DV), F32)],
        scratch_shapes=[pltpu.VMEM((nb, N_DELTA_HEADS, DK, DV), F32)],
        compiler_params=_cparams("parallel", "arbitrary"),
        name="delta_prompt",
    )(conv_in, conv_in, small, z, w_conv, par, norm)


def _delta_sample_kernel(r0_ref, r1_ref, r2_ref, r3_ref, small_ref, z_ref, s_ref, wconv_ref, par_ref, norm_ref,
                         o_ref, sout_ref):
    w = wconv_ref[...]
    conv = r0_ref[...] * w[0:1] + r1_ref[...] * w[1:2] + r2_ref[...] * w[2:3] + r3_ref[...] * w[3:4]
    conv = _silu(conv)
    g_all, beta_all = _decay_beta(small_ref[...], par_ref)
    sb = conv.shape[0]
    eye = (lax.broadcasted_iota(jnp.int32, (DK, DK), 0) == lax.broadcasted_iota(jnp.int32, (DK, DK), 1)).astype(F32)
    for h in range(N_DELTA_HEADS):
        lo = h * DK
        q = _l2n(conv[:, lo:lo + DK], DK ** -0.5)
        k = _l2n(conv[:, DELTA_W + lo:DELTA_W + lo + DK])
        v = conv[:, 2 * DELTA_W + lo:2 * DELTA_W + lo + DV]
        eg = jnp.exp(g_all[:, h:h + 1])
        beta = beta_all[:, N_DELTA_HEADS + h:N_DELTA_HEADS + h + 1]
        qk = jnp.sum(q * k, axis=-1, keepdims=True)
        k_t = _dot_nt(eye, k, HI)
        rows = []
        for b in range(sb):
            s_b = s_ref[b, h]
            ks = _dot(k[b:b + 1, :], s_b, HI)
            qs = _dot(q[b:b + 1, :], s_b, HI)
            v_new = beta[b:b + 1, :] * (v[b:b + 1, :] - eg[b:b + 1, :] * ks)
            rows.append(eg[b:b + 1, :] * qs + qk[b:b + 1, :] * v_new)
            sout_ref[b, h] = s_b * eg[b:b + 1, :] + k_t[:, b:b + 1] * v_new
        o = jnp.concatenate(rows, axis=0)
        o_ref[:, lo:lo + DV] = _rms(o, norm_ref[...]) * _silu(z_ref[:, lo:lo + DV])


def _delta_sample(rows4, small, z, state, w_conv, par, norm, sb):
    s = small.shape[0]
    tile = lambda i: (i, 0)
    const = lambda i: (0, 0)
    st = pl.BlockSpec((sb, N_DELTA_HEADS, DK, DV), lambda i: (i, 0, 0, 0))
    return pl.pallas_call(
        _delta_sample_kernel,
        grid=(s // sb,),
        in_specs=[pl.BlockSpec((sb, CONV_DIM), tile)] * 4
                 + [pl.BlockSpec((sb, LANES), tile), pl.BlockSpec((sb, DELTA_W), tile), st,
                    pl.BlockSpec((CONV_W, CONV_DIM), const), pl.BlockSpec((SUBLANES, LANES), const),
                    pl.BlockSpec((1, DV), const)],
        out_specs=[pl.BlockSpec((sb, DELTA_W), tile), st],
        out_shape=[jax.ShapeDtypeStruct((s, DELTA_W), F32),
                   jax.ShapeDtypeStruct((s, N_DELTA_HEADS, DK, DV), F32)],
        compiler_params=_cparams("parallel"),
        name="delta_sample",
    )(*rows4, small, z, state, w_conv, par, norm)


def _outproj_kernel(x_ref, oa_ref, od_ref, ga_ref, gb_ref, gt1_ref, sh2_ref, sc2_ref, gpost_ref, gpre_ref,
                    wa_ref, wd_ref, wo_ref, x1_ref, h2_ref):
    att = _dot(oa_ref[...].astype(BF16), wa_ref[...])
    dlt = _dot(od_ref[...].astype(BF16), wd_ref[...])
    merged = jax.nn.sigmoid(ga_ref[...]) * att + jax.nn.sigmoid(gb_ref[...]) * dlt
    mix = _dot(merged.astype(BF16), wo_ref[...])
    x1 = x_ref[...] + gt1_ref[...] * _rms(mix, gpost_ref[...])
    x1_ref[...] = x1
    h2_ref[...] = _rms(x1, gpre_ref[...]) * (1.0 + sc2_ref[...]) + sh2_ref[...]


def _outproj(x, o_att, o_d, ga, gb, mods, mod_rows, tiles_per_group, tm, g_post, g_pre_ffn, wa, wd, wo):
    t = x.shape[0]
    tile = lambda i: (i, 0)
    const = lambda i: (0, 0)
    wide = pl.BlockSpec((tm, D_MODEL), tile)
    half = pl.BlockSpec((tm, ATTN_Q_W), tile)
    return pl.pallas_call(
        _outproj_kernel,
        grid=(t // tm,),
        in_specs=[wide, half, half, wide, wide,
                  _mod_spec(mod_rows, 2, tiles_per_group),
                  _mod_spec(mod_rows, 3, tiles_per_group),
                  _mod_spec(mod_rows, 4, tiles_per_group),
                  pl.BlockSpec((1, D_MODEL), const), pl.BlockSpec((1, D_MODEL), const),
                  pl.BlockSpec((ATTN_Q_W, D_MODEL), const), pl.BlockSpec((DELTA_W, D_MODEL), const),
                  pl.BlockSpec((D_MODEL, D_MODEL), const)],
        out_specs=[wide, wide],
        out_shape=[jax.ShapeDtypeStruct((t, D_MODEL), F32)] * 2,
        compiler_params=_cparams("parallel"),
        name="outproj",
    )(x, o_att, o_d, ga, gb, mods, mods, mods, g_post, g_pre_ffn, wa, wd, wo)


def _top_rows(s, k, payload=None):
    r = s.shape[0]
    rows = lax.broadcasted_iota(jnp.int32, s.shape, 0).astype(F32)
    pick_from = rows if payload is None else payload
    vals, picks = [], []
    for _ in range(k):
        m = jnp.max(s, axis=0, keepdims=True)
        first = jnp.min(jnp.where(s == m, rows, float(r)), axis=0, keepdims=True)
        sel = rows == first
        vals.append(m)
        picks.append(first if payload is None else jnp.sum(jnp.where(sel, pick_from, 0.0), axis=0, keepdims=True))
        s = jnp.where(sel, -jnp.inf, s)
    return jnp.concatenate(vals, axis=0), jnp.concatenate(picks, axis=0)


def _pair_candidates(v0, i0, v1, i1):
    k = PEER_TOPK
    assert k == 2 * SUBLANES
    rows = [(slice(0, 1), slice(0, k))]
    rows += [(slice(a, a + 1), slice(0, SUBLANES)) for a in range(1, SUBLANES)]
    rows += [(slice(SUBLANES, k), slice(0, 1))]
    cand = jnp.concatenate([v0[a] + v1[b] for a, b in rows], axis=0)
    cidx = jnp.concatenate([i0[a] * float(N_KEYS) + i1[b] for a, b in rows], axis=0)
    return cand, cidx


def _route_kernel(h_ref, wq_ref, keys_ref, eidx_ref, gate_ref):
    q_t = _dot_nt(wq_ref[...], h_ref[...].astype(BF16))
    k0 = keys_ref[0].astype(BF16)
    k1 = keys_ref[1].astype(BF16)
    eidx, gates = [], []
    for h in range(PEER_HEADS):
        lo = h * 2 * PEER_HALF
        s0 = _dot(k0, q_t[lo:lo + PEER_HALF].astype(BF16))
        s1 = _dot(k1, q_t[lo + PEER_HALF:lo + 2 * PEER_HALF].astype(BF16))
        v0, i0 = _top_rows(s0, PEER_TOPK)
        v1, i1 = _top_rows(s1, PEER_TOPK)
        cand, cidx = _pair_candidates(v0, i0, v1, i1)
        best, e = _top_rows(cand, PEER_TOPK, cidx)
        p = jnp.exp(best - best[0:1])
        gates.append(p / jnp.sum(p, axis=0, keepdims=True))
        eidx.append(e)
    eidx_ref[...] = jnp.concatenate(eidx, axis=0).T.astype(jnp.int32)
    gate_ref[...] = jnp.concatenate(gates, axis=0).T


def _route(h2, wq_t, sub_keys, tm):
    t = h2.shape[0]
    return pl.pallas_call(
        _route_kernel,
        grid=(t // tm,),
        in_specs=[pl.BlockSpec((tm, D_MODEL), lambda i: (i, 0)),
                  pl.BlockSpec(wq_t.shape, lambda i: (0, 0)),
                  pl.BlockSpec(sub_keys.shape, lambda i: (0, 0, 0))],
        out_specs=[pl.BlockSpec((tm, N_SEL), lambda i: (i, 0))] * 2,
        out_shape=[jax.ShapeDtypeStruct((t, N_SEL), jnp.int32), jax.ShapeDtypeStruct((t, N_SEL), F32)],
        compiler_params=_cparams("parallel"),
        name="route",
    )(h2, wq_t, sub_keys)


def _gelu(x):
    return 0.5 * x * (1.0 + lax.erf(x * (2.0 ** -0.5)))


PEER_SLOTS = 3


def _group_sums(tiles):
    sub = lax.broadcasted_iota(jnp.int32, (SUBLANES, LANES), 0)
    m4, m2, m1 = sub < 4, (sub % 4) < 2, (sub % 2) < 1
    q = [jnp.where(m4, tiles[a], tiles[a + 4]) + pltpu.roll(jnp.where(m4, tiles[a + 4], tiles[a]), 4, 0)
         for a in range(4)]
    r = [jnp.where(m2, q[a] + pltpu.roll(q[a], 6, 0), q[a + 1] + pltpu.roll(q[a + 1], 2, 0)) for a in (0, 2)]
    return jnp.where(m1, r[0] + pltpu.roll(r[0], 7, 0), r[1] + pltpu.roll(r[1], 1, 0))


_GROUP_ORDER = (0, 2, 1, 3, 4, 6, 5, 7)


def _peer_kernel(idx_ref, gate_ref, h_ref, x_ref, gt_ref, g_ref, uv_ref, y_ref, buf_ref, cb_ref, f_ref, sem_ref,
                 *, tt, n):
    i = pl.program_id(0)
    cblocks = D_MODEL // LANES
    groups = N_SEL // SUBLANES
    half = N_SEL // 2

    def issue(chunk):
        slot = i % PEER_SLOTS
        t, p0 = chunk // 2, (chunk % 2) * half
        for p in range(p0, p0 + half):
            pltpu.make_async_copy(uv_ref.at[idx_ref[t, p]], buf_ref.at[slot, t * N_SEL + p],
                                  sem_ref.at[slot]).start(priority=p % 2)

    def run(do_issue, do_compute):
        if not do_compute:
            for chunk in range(2 * tt):
                issue(chunk)
            return
        slot = (i + 1) % PEER_SLOTS
        pltpu.make_async_copy(buf_ref.at[slot], buf_ref.at[slot], sem_ref.at[slot]).wait()
        eye = (lax.broadcasted_iota(jnp.int32, (N_SEL, N_SEL), 0)
               == lax.broadcasted_iota(jnp.int32, (N_SEL, N_SEL), 1)).astype(F32)
        gate_t = _dot_nt(eye, gate_ref[...], HI)
        lane = lax.broadcasted_iota(jnp.int32, (N_SEL, tt), 1)
        hmat = jnp.zeros((N_SEL, tt), F32)
        for t in range(tt):
            if do_issue:
                issue(t)
            x8 = jnp.concatenate([h_ref[t:t + 1, c * LANES:(c + 1) * LANES] for c in range(cblocks)], axis=0)
            sums = [_group_sums([buf_ref[slot, t * N_SEL + g * SUBLANES + j, 0:SUBLANES] * x8
                                 for j in _GROUP_ORDER]) for g in range(groups)]
            hcol = jnp.sum(jnp.concatenate(sums, axis=0), axis=-1, keepdims=True)
            hmat = jnp.where(lane == t, hcol, hmat)
        coef = gate_t * _gelu(hmat)
        for t in range(tt):
            if do_issue:
                issue(tt + t)
            cb_ref[...] = jnp.broadcast_to(coef[:, t:t + 1], (N_SEL, LANES))
            parts = []
            for a in range(4):
                acc = cb_ref[a:a + 1, :] * buf_ref[slot, t * N_SEL + a, SUBLANES:]
                for p in range(a + 4, N_SEL, 4):
                    acc = acc + cb_ref[p:p + 1, :] * buf_ref[slot, t * N_SEL + p, SUBLANES:]
                parts.append(acc)
            f8 = (parts[0] + parts[1]) + (parts[2] + parts[3])
            for c in range(cblocks):
                f_ref[t:t + 1, c * LANES:(c + 1) * LANES] = f8[c:c + 1, :]
        y_ref[...] = x_ref[...] + gt_ref[...] * _rms(f_ref[...], g_ref[...])

    lag = PEER_SLOTS - 1

    @pl.when(i < min(lag, n))
    def _():
        run(True, False)

    if n > lag:
        @pl.when(jnp.logical_and(i >= lag, i < n))
        def _():
            run(True, True)

    @pl.when(i >= max(lag, n))
    def _():
        run(False, True)


def _peer(eidx, gate, h2, x1, mods, mod_rows, tiles_per_group, tt, g_post, uv):
    t = h2.shape[0]
    n = t // tt
    lag = PEER_SLOTS - 1
    done = lambda i: (jnp.maximum(i - lag, 0), 0)
    return pl.pallas_call(
        functools.partial(_peer_kernel, tt=tt, n=n),
        grid=(n + lag,),
        in_specs=[pl.BlockSpec((tt, N_SEL), lambda i: (jnp.minimum(i, n - 1), 0), memory_space=pltpu.SMEM),
                  pl.BlockSpec((tt, N_SEL), done),
                  pl.BlockSpec((tt, D_MODEL), done),
                  pl.BlockSpec((tt, D_MODEL), done),
                  pl.BlockSpec((None, mod_rows, D_MODEL),
                               lambda i: (jnp.maximum(i - lag, 0) // tiles_per_group, 0, N_ADA - 1)),
                  pl.BlockSpec((1, D_MODEL), lambda i: (0, 0)),
                  pl.BlockSpec(memory_space=pl.ANY)],
        out_specs=pl.BlockSpec((tt, D_MODEL), done),
        out_shape=jax.ShapeDtypeStruct((t, D_MODEL), F32),
        scratch_shapes=[pltpu.VMEM((PEER_SLOTS, tt * N_SEL, 2 * D_MODEL // LANES, LANES), F32),
                        pltpu.VMEM((N_SEL, LANES), F32),
                        pltpu.VMEM((tt, D_MODEL), F32),
                        pltpu.SemaphoreType.DMA((PEER_SLOTS,))],
        compiler_params=_cparams("arbitrary"),
        name="peer",
    )(eidx, gate, h2, x1, mods, g_post, uv)


def _token_path(x, mods, mod_rows, tiles_per_group, tm, tt, mixer, p):
    attn, conv_in, small, z, ga, gb = _inproj(x, mods, mod_rows, tiles_per_group, tm, p["g_pre_mix"], p["w_in_parts"])
    o_att, o_d, extras = mixer(attn, conv_in, small, z)
    x1, h2 = _outproj(x, o_att, o_d, ga, gb, mods, mod_rows, tiles_per_group, tm, p["g_post_mix"], p["g_pre_ffn"],
                      p["w_att_out"], p["w_delta_out"], p["w_out"])
    eidx, gate = _route(h2, p["wq_t"], p["sub_keys"], tm)
    if mod_rows == 1:
        y = _peer(eidx, gate, h2, x1, mods, 1, tiles_per_group * tm // tt, tt, p["g_post_ffn"], p["uv"])
    else:
        y = _peer(eidx, gate, h2, x1, mods.reshape(-1, tt, N_ADA * D_MODEL), tt, 1, tt, p["g_post_ffn"], p["uv"])
    return y, extras


def _bias_tables(rel_bias):
    qi = jnp.arange(WINDOW, dtype=jnp.int32)[:, None]
    kj = jnp.arange(2 * WINDOW, dtype=jnp.int32)[None, :]
    dist = qi + WINDOW - kj
    valid = (dist >= 0) & (dist < WINDOW)
    bias = jnp.moveaxis(rel_bias[_rel_bucket(dist)], -1, 0)
    later = jnp.where(valid[None], bias, NEG_BIG)
    first = jnp.where((valid & (kj >= WINDOW))[None], bias, NEG_BIG)
    dist_s = WINDOW - 1 - jnp.arange(WINDOW, dtype=jnp.int32)
    bias_s = rel_bias[_rel_bucket(dist_s)].T
    return jnp.stack([first, later]), bias_s


def _layer(xp, xs, c_all, k_buf, v_buf, conv_buf, s0, rel_bias, lp):
    batch, seq, _ = xp.shape
    dec = xs.shape[0]
    assert xs.shape[1] == 1 and seq % WINDOW == 0 and k_buf.shape[1] == WINDOW

    ada = _ada(c_all, lp["w_ada"], lp["b_ada"])
    mods_p = ada[:batch].reshape(batch, 1, N_ADA * D_MODEL)
    mods_s = ada[batch:].reshape(1, dec, N_ADA * D_MODEL)

    w_in = lp["w_in"]
    cuts = (0, ATTN_W, ATTN_W + CONV_DIM, ATTN_W + CONV_DIM + 2 * N_DELTA_HEADS)
    cuts = cuts + (cuts[-1] + DELTA_W, cuts[-1] + DELTA_W + D_MODEL, cuts[-1] + DELTA_W + 2 * D_MODEL)
    parts = [w_in[:, a:b] for a, b in zip(cuts[:-1], cuts[1:])]
    parts[2] = jnp.pad(parts[2], ((0, 0), (0, LANES - 2 * N_DELTA_HEADS)))
    row = lambda v: v.reshape(1, -1)
    par = jnp.zeros((SUBLANES, LANES), F32).at[0, :N_DELTA_HEADS].set(lp["a_log"]).at[1, :N_DELTA_HEADS].set(lp["dt_bias"])
    p = dict(
        w_in_parts=[w.astype(BF16) for w in parts],
        g_pre_mix=row(lp["g_pre_mix"]), g_post_mix=row(lp["g_post_mix"]),
        g_pre_ffn=row(lp["g_pre_ffn"]), g_post_ffn=row(lp["g_post_ffn"]),
        w_att_out=lp["w_att_out"].astype(BF16), w_delta_out=lp["w_delta_out"].astype(BF16),
        w_out=lp["w_out"].astype(BF16),
        wq_t=lp["w_query"].T.astype(BF16), sub_keys=lp["sub_keys"],
        uv=jnp.concatenate([lp["expert_u"], lp["expert_v"]], axis=1).reshape(-1, 2 * D_MODEL // LANES, LANES),
    )
    norm = row(lp["delta_norm"])
    bias2, bias_s = _bias_tables(rel_bias)
    out = {}

    def mixer_prompt(attn, conv_in, small, z):
        o_att = _swa_prompt(attn, batch, seq, lp["attn_sinks"], bias2)
        nb = 2 if batch % 2 == 0 else 1
        o_d, s_new = _delta_prompt(conv_in.reshape(batch, seq, CONV_DIM), small.reshape(batch, seq, LANES),
                                   z.reshape(batch, seq, DELTA_W), batch, seq, lp["w_conv"], par, norm, nb)
        o_d = o_d.reshape(batch * seq, DELTA_W)
        a3 = attn.reshape(batch, seq, ATTN_W)
        new_k = a3[:, -WINDOW:, ATTN_Q_W:ATTN_Q_W + ATTN_KV_W].reshape(batch, WINDOW, N_KV_HEADS, HEAD_DIM)
        new_v = a3[:, -WINDOW:, ATTN_Q_W + ATTN_KV_W:].reshape(batch, WINDOW, N_KV_HEADS, HEAD_DIM)
        new_conv = conv_in.reshape(batch, seq, CONV_DIM)[:, -(CONV_W - 1):]
        return o_att, o_d, (new_k, new_v, new_conv, s_new)

    def mixer_sample(attn, conv_in, small, z):
        k_new = attn[:, ATTN_Q_W:ATTN_Q_W + ATTN_KV_W].reshape(dec, 1, N_KV_HEADS, HEAD_DIM)
        v_new = attn[:, ATTN_Q_W + ATTN_KV_W:].reshape(dec, 1, N_KV_HEADS, HEAD_DIM)
        new_k = jnp.concatenate([k_buf[:, 1:], k_new], axis=1)
        new_v = jnp.concatenate([v_buf[:, 1:], v_new], axis=1)
        sb = SUBLANES if dec % SUBLANES == 0 else dec
        o_att = _swa_sample(attn[:, :ATTN_Q_W].reshape(dec, N_Q_HEADS, HEAD_DIM),
                            new_k.reshape(dec, WINDOW, ATTN_KV_W), new_v.reshape(dec, WINDOW, ATTN_KV_W),
                            lp["attn_sinks"], bias_s, sb).reshape(dec, ATTN_Q_W)
        rows4 = [conv_buf[:, i] for i in range(CONV_W - 1)] + [conv_in]
        o_d, s_new = _delta_sample(rows4, small, z, s0, lp["w_conv"], par, norm, sb)
        new_conv = jnp.concatenate([conv_buf[:, 1:], conv_in[:, None]], axis=1)
        return o_att, o_d, (new_k, new_v, new_conv, s_new)

    tm_s = dec
    ys, ex_s = _token_path(xs.reshape(dec, D_MODEL), mods_s, dec, 1, tm_s, SUBLANES, mixer_sample, p)
    tm_p = 256 if seq % 256 == 0 else WINDOW
    yp, ex_p = _token_path(xp.reshape(batch * seq, D_MODEL), mods_p, 1, seq // tm_p, tm_p, SUBLANES, mixer_prompt, p)
    return yp.reshape(batch, seq, D_MODEL), ys.reshape(dec, 1, D_MODEL), ex_p, ex_s


def kernel(x_prompt, x_sample, cache_swa_k, cache_swa_v, state_conv, state_delta, c_prompt, c_sample, rel_bias, w_ada, b_ada, g_pre_mix, g_post_mix, g_pre_ffn, g_post_ffn, w_in, attn_sinks, w_conv, a_log, dt_bias, delta_norm, w_att_out, w_delta_out, w_out, w_query, sub_keys, expert_u, expert_v):
    depth = w_in.shape[0]
    yp, ys = x_prompt, x_sample
    c_all = jnp.concatenate([c_prompt, c_sample], axis=0)
    ex_p_all, ex_s_all = [], []
    for layer in range(depth):
        lp = dict(w_ada=w_ada[layer], b_ada=b_ada[layer], g_pre_mix=g_pre_mix[layer], g_post_mix=g_post_mix[layer],
                  g_pre_ffn=g_pre_ffn[layer], g_post_ffn=g_post_ffn[layer], w_in=w_in[layer],
                  attn_sinks=attn_sinks[layer], w_conv=w_conv[layer], a_log=a_log[layer], dt_bias=dt_bias[layer],
                  delta_norm=delta_norm[layer], w_att_out=w_att_out[layer], w_delta_out=w_delta_out[layer],
                  w_out=w_out[layer], w_query=w_query[layer], sub_keys=sub_keys[layer],
                  expert_u=expert_u[layer], expert_v=expert_v[layer])
        yp, ys, ex_p, ex_s = _layer(yp, ys, c_all, cache_swa_k[layer], cache_swa_v[layer], state_conv[layer],
                                    state_delta[layer], rel_bias, lp)
        ex_p_all.append(ex_p)
        ex_s_all.append(ex_s)
    stack = lambda exs, j: jnp.stack([e[j] for e in exs])
    return (yp, ys, stack(ex_p_all, 0), stack(ex_p_all, 1), stack(ex_p_all, 2), stack(ex_p_all, 3),
            stack(ex_s_all, 0), stack(ex_s_all, 1), stack(ex_s_all, 2), stack(ex_s_all, 3))
```

```python
import functools
import math

import jax
import jax.numpy as jnp
from jax import lax
from jax.experimental import pallas as pl
from jax.experimental.pallas import tpu as pltpu

F32 = jnp.float32
BF16 = jnp.bfloat16
HI = lax.Precision.HIGHEST

D_MODEL = 1024
N_Q_HEADS = 8
N_KV_HEADS = 2
Q_PER_KV = N_Q_HEADS // N_KV_HEADS
HEAD_DIM = 64
WINDOW = 128
N_BUCKETS = 32
MAX_DISTANCE = 128
N_DELTA_HEADS = 4
DK = 128
DV = 128
CONV_W = 4
CHUNK = 64
N_KEYS = 128
PEER_HEADS = 8
PEER_HALF = 128
PEER_TOPK = 16
RMS_EPS = 1e-6
N_ADA = 6
ATTN_Q_W = N_Q_HEADS * HEAD_DIM
ATTN_KV_W = N_KV_HEADS * HEAD_DIM
ATTN_W = ATTN_Q_W + 2 * ATTN_KV_W
DELTA_W = N_DELTA_HEADS * DK
CONV_DIM = 3 * DELTA_W
N_SEL = PEER_HEADS * PEER_TOPK
LANES = 128
SUBLANES = 8
NEG_BIG = -1e30
VMEM_LIMIT = 56 * 1024 * 1024


def _cparams(*sem):
    return pltpu.CompilerParams(dimension_semantics=sem, vmem_limit_bytes=VMEM_LIMIT)


def _rms(x, g):
    return x * lax.rsqrt(jnp.mean(x * x, axis=-1, keepdims=True) + RMS_EPS) * g


def _silu(x):
    return x * jax.nn.sigmoid(x)


def _dot(a, b, precision=None):
    return jnp.dot(a, b, preferred_element_type=F32, precision=precision)


def _dot_nt(a, b, precision=None):
    return lax.dot_general(a, b, (((1,), (1,)), ((), ())), preferred_element_type=F32, precision=precision)


def _ada_kernel(c_ref, w_ref, b_ref, o_ref):
    o_ref[...] = _dot(_silu(c_ref[...]), w_ref[...], HI) + b_ref[...]


def _ada(c_all, w_ada, b_ada):
    rows = c_all.shape[0]
    n_out = w_ada.shape[1]
    tn = 512
    return pl.pallas_call(
        _ada_kernel,
        grid=(n_out // tn,),
        in_specs=[pl.BlockSpec((rows, D_MODEL), lambda j: (0, 0)),
                  pl.BlockSpec((D_MODEL, tn), lambda j: (0, j)),
                  pl.BlockSpec((1, tn), lambda j: (0, j))],
        out_specs=pl.BlockSpec((rows, tn), lambda j: (0, j)),
        out_shape=jax.ShapeDtypeStruct((rows, n_out), F32),
        compiler_params=_cparams("parallel"),
        name="ada",
    )(c_all, w_ada, b_ada.reshape(1, n_out))


def _mod_spec(rows, chunk, tiles_per_group):
    return pl.BlockSpec((None, rows, D_MODEL), lambda i: (i // tiles_per_group, 0, chunk))


def _inproj_kernel(x_ref, sh_ref, sc_ref, g_ref, wa_ref, wc_ref, ws_ref, wz_ref, wga_ref, wgb_ref,
                   oa_ref, oc_ref, os_ref, oz_ref, oga_ref, ogb_ref):
    h = (_rms(x_ref[...], g_ref[...]) * (1.0 + sc_ref[...]) + sh_ref[...]).astype(BF16)
    for w_ref, o_ref in ((wa_ref, oa_ref), (wc_ref, oc_ref), (ws_ref, os_ref), (wz_ref, oz_ref),
                         (wga_ref, oga_ref), (wgb_ref, ogb_ref)):
        o_ref[...] = _dot(h, w_ref[...])


def _inproj(x, mods, mod_rows, tiles_per_group, tm, g_pre, weights):
    t = x.shape[0]
    widths = [w.shape[1] for w in weights]
    const = lambda i: (0, 0)
    return pl.pallas_call(
        _inproj_kernel,
        grid=(t // tm,),
        in_specs=[pl.BlockSpec((tm, D_MODEL), lambda i: (i, 0)),
                  _mod_spec(mod_rows, 0, tiles_per_group),
                  _mod_spec(mod_rows, 1, tiles_per_group),
                  pl.BlockSpec((1, D_MODEL), const)]
                 + [pl.BlockSpec((D_MODEL, n), const) for n in widths],
        out_specs=[pl.BlockSpec((tm, n), lambda i: (i, 0)) for n in widths],
        out_shape=[jax.ShapeDtypeStruct((t, n), F32) for n in widths],
        compiler_params=_cparams("parallel"),
        name="inproj",
    )(x, mods, mods, g_pre, *weights)


def _rel_bucket(dist):
    n = jnp.maximum(dist, 0)
    max_exact = N_BUCKETS // 2
    nf = jnp.maximum(n, 1).astype(F32)
    large = max_exact + (jnp.log(nf / max_exact) / math.log(MAX_DISTANCE / max_exact)
                         * (N_BUCKETS - max_exact)).astype(jnp.int32)
    return jnp.where(n < max_exact, n, jnp.minimum(large, N_BUCKETS - 1))


def _softmax_sink(s, sink):
    m = jnp.maximum(jnp.max(s, axis=-1, keepdims=True), sink)
    p = jnp.exp(s - m)
    denom = jnp.sum(p, axis=-1, keepdims=True) + jnp.exp(sink - m)
    return p / denom


def _swa_prompt_kernel(sink_ref, q_ref, kp_ref, kc_ref, vp_ref, vc_ref, bias_ref, o_ref):
    q = q_ref[...] * (HEAD_DIM ** -0.5)
    k2 = jnp.concatenate([kp_ref[...], kc_ref[...]], axis=0).astype(BF16)
    v2 = jnp.concatenate([vp_ref[...], vc_ref[...]], axis=0).astype(BF16)
    for h in range(N_Q_HEADS):
        g = h // Q_PER_KV
        qh = q[:, h * HEAD_DIM:(h + 1) * HEAD_DIM].astype(BF16)
        s = _dot_nt(qh, k2[:, g * HEAD_DIM:(g + 1) * HEAD_DIM]) + bias_ref[h]
        p = _softmax_sink(s, sink_ref[h])
        o_ref[:, h * HEAD_DIM:(h + 1) * HEAD_DIM] = _dot(p.astype(BF16), v2[:, g * HEAD_DIM:(g + 1) * HEAD_DIM])


def _swa_prompt(attn, batch, seq, sinks, bias2):
    nb = seq // WINDOW
    qcol = ATTN_Q_W // ATTN_KV_W
    cur = lambda c: (lambda b, j: (b * nb + j, c))
    prev = lambda c: (lambda b, j: (b * nb + jnp.maximum(j - 1, 0), c))
    return pl.pallas_call(
        _swa_prompt_kernel,
        grid=(batch, nb),
        in_specs=[pl.BlockSpec(memory_space=pltpu.SMEM),
                  pl.BlockSpec((WINDOW, ATTN_Q_W), lambda b, j: (b * nb + j, 0)),
                  pl.BlockSpec((WINDOW, ATTN_KV_W), prev(qcol)),
                  pl.BlockSpec((WINDOW, ATTN_KV_W), cur(qcol)),
                  pl.BlockSpec((WINDOW, ATTN_KV_W), prev(qcol + 1)),
                  pl.BlockSpec((WINDOW, ATTN_KV_W), cur(qcol + 1)),
                  pl.BlockSpec((None, N_Q_HEADS, WINDOW, 2 * WINDOW), lambda b, j: (jnp.minimum(j, 1), 0, 0, 0))],
        out_specs=pl.BlockSpec((WINDOW, ATTN_Q_W), lambda b, j: (b * nb + j, 0)),
        out_shape=jax.ShapeDtypeStruct((batch * seq, ATTN_Q_W), F32),
        compiler_params=_cparams("parallel", "parallel"),
        name="swa_prompt",
    )(sinks, attn, attn, attn, attn, attn, bias2)


def _swa_sample_kernel(sink_ref, q_ref, k_ref, v_ref, bias_ref, o_ref):
    q = q_ref[...] * (HEAD_DIM ** -0.5)
    q2 = jnp.concatenate([q, q], axis=-1)
    head = lax.broadcasted_iota(jnp.int32, q2.shape, 1)
    lane = lax.broadcasted_iota(jnp.int32, q2.shape, 2)
    qm = jnp.where(lane // HEAD_DIM == head // Q_PER_KV, q2, 0.0).astype(BF16)
    s = jnp.einsum("shc,sjc->shj", qm, k_ref[...].astype(BF16), preferred_element_type=F32)
    p = _softmax_sink(s + bias_ref[...][None], sink_ref[...][None])
    o2 = jnp.einsum("shj,sjc->shc", p.astype(BF16), v_ref[...].astype(BF16), preferred_element_type=F32)
    head_o = lax.broadcasted_iota(jnp.int32, o_ref.shape, 1)
    o_ref[...] = jnp.where(head_o < Q_PER_KV, o2[:, :, :HEAD_DIM], o2[:, :, HEAD_DIM:])


def _swa_sample(q, k_win, v_win, sinks, bias_s, sb):
    s = q.shape[0]
    return pl.pallas_call(
        _swa_sample_kernel,
        grid=(s // sb,),
        in_specs=[pl.BlockSpec((N_Q_HEADS, 1), lambda i: (0, 0)),
                  pl.BlockSpec((sb, N_Q_HEADS, HEAD_DIM), lambda i: (i, 0, 0)),
                  pl.BlockSpec((sb, WINDOW, ATTN_KV_W), lambda i: (i, 0, 0)),
                  pl.BlockSpec((sb, WINDOW, ATTN_KV_W), lambda i: (i, 0, 0)),
                  pl.BlockSpec((N_Q_HEADS, WINDOW), lambda i: (0, 0))],
        out_specs=pl.BlockSpec((sb, N_Q_HEADS, HEAD_DIM), lambda i: (i, 0, 0)),
        out_shape=jax.ShapeDtypeStruct((s, N_Q_HEADS, HEAD_DIM), F32),
        compiler_params=_cparams("parallel"),
        name="swa_sample",
    )(sinks.reshape(N_Q_HEADS, 1), q, k_win, v_win, bias_s)


def _l2n(x, scale=1.0):
    return x * (lax.rsqrt(jnp.sum(x * x, axis=-1, keepdims=True) + RMS_EPS) * scale)


def _softplus(x):
    return jnp.maximum(x, 0.0) + jnp.log(1.0 + jnp.exp(-jnp.abs(x)))


def _decay_beta(small, par_ref):
    g = -jnp.exp(par_ref[0:1, :]) * _softplus(small + par_ref[1:2, :])
    return g, jax.nn.sigmoid(small)


def _split(x):
    hi = x.astype(BF16)
    return hi, (x - hi.astype(F32)).astype(BF16)


def _dot3(a, b, nt=False):
    mm = _dot_nt if nt else _dot
    return mm(a[0], b[0]) + (mm(a[0], b[1]) + mm(a[1], b[0]))


def _unit_lower_inverse(lm):
    c = lm.shape[0]
    eye = (lax.broadcasted_iota(jnp.int32, (c, c), 0) == lax.broadcasted_iota(jnp.int32, (c, c), 1)).astype(F32)
    power = -lm
    total = eye + power
    k = 2
    while k <= c:
        ps = _split(power)
        if k == 2:
            power = _dot3(ps, ps)
        else:
            total = total + _dot3(ps, _split(total))
            if k < c:
                power = _dot3(ps, ps)
        k *= 2
        yield
    return total


def _delta_head(conv, gc_col, gc_row, beta_all, h, z_ref_b, norm, s_ref_b, o_ref_b, masks):
    c = CHUNK
    causal, strict = masks
    lo = h * DK
    q = _l2n(conv[:, lo:lo + DK], DK ** -0.5)
    k = _l2n(conv[:, DELTA_W + lo:DELTA_W + lo + DK])
    v = conv[:, 2 * DELTA_W + lo:2 * DELTA_W + lo + DV]
    gcol = gc_col[:, h:h + 1]
    grow = gc_row[h:h + 1, :]
    beta = beta_all[:, N_DELTA_HEADS + h:N_DELTA_HEADS + h + 1]
    decay = jnp.where(causal, jnp.exp(jnp.where(causal, gcol - grow, 0.0)), 0.0)
    kb = k * beta
    kk = _dot3(_split(jnp.concatenate([kb, q], axis=0)), _split(k), nt=True)
    yield
    lm = jnp.where(strict, kk[:c] * decay, 0.0)
    a_intra = jnp.where(causal, kk[c:] * decay, 0.0)
    tmat = yield from _unit_lower_inverse(lm)
    uw = _dot3(_split(tmat), _split(jnp.concatenate([v * beta, kb * jnp.exp(gcol)], axis=1)))
    yield
    s_h = s_ref_b[h]
    ws_qs = _dot3(_split(jnp.concatenate([uw[:, DV:], q * jnp.exp(gcol)], axis=0)), _split(s_h))
    yield
    v_new = uw[:, :DV] - ws_qs[:c]
    v_new_s = _split(v_new)
    o = ws_qs[c:] + _dot3(_split(a_intra), v_new_s)
    g_last = gcol[c - 1:c, :]
    s_ref_b[h] = s_h * jnp.exp(g_last) + _dot3(_split((k * jnp.exp(g_last - gcol)).T), v_new_s)
    o_ref_b[:, lo:lo + DV] = _rms(o, norm) * _silu(z_ref_b[:, lo:lo + DV])


def _delta_prompt_kernel(prev_ref, cur_ref, small_ref, z_ref, wconv_ref, par_ref, norm_ref,
                         o_ref, sout_ref, s_ref, *, nb):
    n = pl.program_id(1)
    c = CHUNK

    @pl.when(n == 0)
    def _():
        s_ref[...] = jnp.zeros_like(s_ref)

    ri = lax.broadcasted_iota(jnp.int32, (c, c), 0)
    ci = lax.broadcasted_iota(jnp.int32, (c, c), 1)
    tri_lo = (ri >= ci).astype(F32)
    tri_up = (ri <= ci).astype(F32)
    w = wconv_ref[...]
    chains = []
    for b in range(nb):
        prev = jnp.where(n > 0, prev_ref[b], 0.0)
        full = jnp.concatenate([prev, cur_ref[b]], axis=0)
        base = SUBLANES - (CONV_W - 1)
        conv = full[base:base + c] * w[0:1]
        for i in range(1, CONV_W):
            conv = conv + full[base + i:base + i + c] * w[i:i + 1]
        conv = _silu(conv)
        g_all, beta_all = _decay_beta(small_ref[b], par_ref)
        gc_col = _dot(tri_lo, g_all, HI)
        gc_row = _dot(g_all.T, tri_up, HI)
        chains += [_delta_head(conv, gc_col, gc_row, beta_all, h, z_ref.at[b], norm_ref[...], s_ref.at[b],
                               o_ref.at[b], (ri >= ci, ri > ci)) for h in range(N_DELTA_HEADS)]
    while chains:
        chains = [ch for ch in chains if next(ch, chains) is not chains]

    @pl.when(n == pl.num_programs(1) - 1)
    def _():
        sout_ref[...] = s_ref[...]


def _delta_prompt(conv_in, small, z, batch, seq, w_conv, par, norm, nb):
    nc = seq // CHUNK
    per = CHUNK // SUBLANES
    tile = lambda b, n: (b, n, 0)
    const = lambda b, n: (0, 0)
    return pl.pallas_call(
        functools.partial(_delta_prompt_kernel, nb=nb),
        grid=(batch // nb, nc),
        in_specs=[pl.BlockSpec((nb, SUBLANES, CONV_DIM), lambda b, n: (b, jnp.maximum(n * per - 1, 0), 0)),
                  pl.BlockSpec((nb, CHUNK, CONV_DIM), tile),
                  pl.BlockSpec((nb, CHUNK, LANES), tile),
                  pl.BlockSpec((nb, CHUNK, DELTA_W), tile),
                  pl.BlockSpec((CONV_W, CONV_DIM), const),
                  pl.BlockSpec((SUBLANES, LANES), const),
                  pl.BlockSpec((1, DV), const)],
        out_specs=[pl.BlockSpec((nb, CHUNK, DELTA_W), tile),
                   pl.BlockSpec((nb, N_DELTA_HEADS, DK, DV), lambda b, n: (b, 0, 0, 0))],
        out_shape=[jax.ShapeDtypeStruct((batch, seq, DELTA_W), F32),
                   jax.ShapeDtypeStruct((batch, N_DELTA_HEADS, DK, DV), F32)],
        scratch_shapes=[pltpu.VMEM((nb, N_DELTA_HEADS, DK, DV), F32)],
        compiler_params=_cparams("parallel", "arbitrary"),
        name="delta_prompt",
    )(conv_in, conv_in, small, z, w_conv, par, norm)


def _delta_sample_kernel(r0_ref, r1_ref, r2_ref, r3_ref, small_ref, z_ref, s_ref, wconv_ref, par_ref, norm_ref,
                         o_ref, sout_ref):
    w = wconv_ref[...]
    conv = r0_ref[...] * w[0:1] + r1_ref[...] * w[1:2] + r2_ref[...] * w[2:3] + r3_ref[...] * w[3:4]
    conv = _silu(conv)
    g_all, beta_all = _decay_beta(small_ref[...], par_ref)
    sb = conv.shape[0]
    eye = (lax.broadcasted_iota(jnp.int32, (DK, DK), 0) == lax.broadcasted_iota(jnp.int32, (DK, DK), 1)).astype(F32)
    for h in range(N_DELTA_HEADS):
        lo = h * DK
        q = _l2n(conv[:, lo:lo + DK], DK ** -0.5)
        k = _l2n(conv[:, DELTA_W + lo:DELTA_W + lo + DK])
        v = conv[:, 2 * DELTA_W + lo:2 * DELTA_W + lo + DV]
        eg = jnp.exp(g_all[:, h:h + 1])
        beta = beta_all[:, N_DELTA_HEADS + h:N_DELTA_HEADS + h + 1]
        qk = jnp.sum(q * k, axis=-1, keepdims=True)
        k_t = _dot_nt(eye, k, HI)
        rows = []
        for b in range(sb):
            s_b = s_ref[b, h]
            ks = _dot(k[b:b + 1, :], s_b, HI)
            qs = _dot(q[b:b + 1, :], s_b, HI)
            v_new = beta[b:b + 1, :] * (v[b:b + 1, :] - eg[b:b + 1, :] * ks)
            rows.append(eg[b:b + 1, :] * qs + qk[b:b + 1, :] * v_new)
            sout_ref[b, h] = s_b * eg[b:b + 1, :] + k_t[:, b:b + 1] * v_new
        o = jnp.concatenate(rows, axis=0)
        o_ref[:, lo:lo + DV] = _rms(o, norm_ref[...]) * _silu(z_ref[:, lo:lo + DV])


def _delta_sample(rows4, small, z, state, w_conv, par, norm, sb):
    s = small.shape[0]
    tile = lambda i: (i, 0)
    const = lambda i: (0, 0)
    st = pl.BlockSpec((sb, N_DELTA_HEADS, DK, DV), lambda i: (i, 0, 0, 0))
    return pl.pallas_call(
        _delta_sample_kernel,
        grid=(s // sb,),
        in_specs=[pl.BlockSpec((sb, CONV_DIM), tile)] * 4
                 + [pl.BlockSpec((sb, LANES), tile), pl.BlockSpec((sb, DELTA_W), tile), st,
                    pl.BlockSpec((CONV_W, CONV_DIM), const), pl.BlockSpec((SUBLANES, LANES), const),
                    pl.BlockSpec((1, DV), const)],
        out_specs=[pl.BlockSpec((sb, DELTA_W), tile), st],
        out_shape=[jax.ShapeDtypeStruct((s, DELTA_W), F32),
                   jax.ShapeDtypeStruct((s, N_DELTA_HEADS, DK, DV), F32)],
        compiler_params=_cparams("parallel"),
        name="delta_sample",
    )(*rows4, small, z, state, w_conv, par, norm)


def _outproj_kernel(x_ref, oa_ref, od_ref, ga_ref, gb_ref, gt1_ref, sh2_ref, sc2_ref, gpost_ref, gpre_ref,
                    wa_ref, wd_ref, wo_ref, x1_ref, h2_ref):
    att = _dot(oa_ref[...].astype(BF16), wa_ref[...])
    dlt = _dot(od_ref[...].astype(BF16), wd_ref[...])
    merged = jax.nn.sigmoid(ga_ref[...]) * att + jax.nn.sigmoid(gb_ref[...]) * dlt
    mix = _dot(merged.astype(BF16), wo_ref[...])
    x1 = x_ref[...] + gt1_ref[...] * _rms(mix, gpost_ref[...])
    x1_ref[...] = x1
    h2_ref[...] = _rms(x1, gpre_ref[...]) * (1.0 + sc2_ref[...]) + sh2_ref[...]


def _outproj(x, o_att, o_d, ga, gb, mods, mod_rows, tiles_per_group, tm, g_post, g_pre_ffn, wa, wd, wo):
    t = x.shape[0]
    tile = lambda i: (i, 0)
    const = lambda i: (0, 0)
    wide = pl.BlockSpec((tm, D_MODEL), tile)
    half = pl.BlockSpec((tm, ATTN_Q_W), tile)
    return pl.pallas_call(
        _outproj_kernel,
        grid=(t // tm,),
        in_specs=[wide, half, half, wide, wide,
                  _mod_spec(mod_rows, 2, tiles_per_group),
                  _mod_spec(mod_rows, 3, tiles_per_group),
                  _mod_spec(mod_rows, 4, tiles_per_group),
                  pl.BlockSpec((1, D_MODEL), const), pl.BlockSpec((1, D_MODEL), const),
                  pl.BlockSpec((ATTN_Q_W, D_MODEL), const), pl.BlockSpec((DELTA_W, D_MODEL), const),
                  pl.BlockSpec((D_MODEL, D_MODEL), const)],
        out_specs=[wide, wide],
        out_shape=[jax.ShapeDtypeStruct((t, D_MODEL), F32)] * 2,
        compiler_params=_cparams("parallel"),
        name="outproj",
    )(x, o_att, o_d, ga, gb, mods, mods, mods, g_post, g_pre_ffn, wa, wd, wo)


def _top_rows(s, k, payload=None):
    r = s.shape[0]
    rows = lax.broadcasted_iota(jnp.int32, s.shape, 0).astype(F32)
    pick_from = rows if payload is None else payload
    vals, picks = [], []
    for _ in range(k):
        m = jnp.max(s, axis=0, keepdims=True)
        first = jnp.min(jnp.where(s == m, rows, float(r)), axis=0, keepdims=True)
        sel = rows == first
        vals.append(m)
        picks.append(first if payload is None else jnp.sum(jnp.where(sel, pick_from, 0.0), axis=0, keepdims=True))
        s = jnp.where(sel, -jnp.inf, s)
    return jnp.concatenate(vals, axis=0), jnp.concatenate(picks, axis=0)


def _pair_candidates(v0, i0, v1, i1):
    k = PEER_TOPK
    assert k == 2 * SUBLANES
    rows = [(slice(0, 1), slice(0, k))]
    rows += [(slice(a, a + 1), slice(0, SUBLANES)) for a in range(1, SUBLANES)]
    rows += [(slice(SUBLANES, k), slice(0, 1))]
    cand = jnp.concatenate([v0[a] + v1[b] for a, b in rows], axis=0)
    cidx = jnp.concatenate([i0[a] * float(N_KEYS) + i1[b] for a, b in rows], axis=0)
    return cand, cidx


def _route_kernel(h_ref, wq_ref, keys_ref, eidx_ref, gate_ref):
    q_t = _dot_nt(wq_ref[...], h_ref[...].astype(BF16))
    k0 = keys_ref[0].astype(BF16)
    k1 = keys_ref[1].astype(BF16)
    eidx, gates = [], []
    for h in range(PEER_HEADS):
        lo = h * 2 * PEER_HALF
        s0 = _dot(k0, q_t[lo:lo + PEER_HALF].astype(BF16))
        s1 = _dot(k1, q_t[lo + PEER_HALF:lo + 2 * PEER_HALF].astype(BF16))
        v0, i0 = _top_rows(s0, PEER_TOPK)
        v1, i1 = _top_rows(s1, PEER_TOPK)
        cand, cidx = _pair_candidates(v0, i0, v1, i1)
        best, e = _top_rows(cand, PEER_TOPK, cidx)
        p = jnp.exp(best - best[0:1])
        gates.append(p / jnp.sum(p, axis=0, keepdims=True))
        eidx.append(e)
    eidx_ref[...] = jnp.concatenate(eidx, axis=0).T.astype(jnp.int32)
    gate_ref[...] = jnp.concatenate(gates, axis=0).T


def _route(h2, wq_t, sub_keys, tm):
    t = h2.shape[0]
    return pl.pallas_call(
        _route_kernel,
        grid=(t // tm,),
        in_specs=[pl.BlockSpec((tm, D_MODEL), lambda i: (i, 0)),
                  pl.BlockSpec(wq_t.shape, lambda i: (0, 0)),
                  pl.BlockSpec(sub_keys.shape, lambda i: (0, 0, 0))],
        out_specs=[pl.BlockSpec((tm, N_SEL), lambda i: (i, 0))] * 2,
        out_shape=[jax.ShapeDtypeStruct((t, N_SEL), jnp.int32), jax.ShapeDtypeStruct((t, N_SEL), F32)],
        compiler_params=_cparams("parallel"),
        name="route",
    )(h2, wq_t, sub_keys)


def _gelu(x):
    return 0.5 * x * (1.0 + lax.erf(x * (2.0 ** -0.5)))


PEER_SLOTS = 3
ROW_TILE = 2 * D_MODEL // LANES


def _group_sums(tiles):
    sub = lax.broadcasted_iota(jnp.int32, (SUBLANES, LANES), 0)
    m4, m2, m1 = sub < 4, (sub % 4) < 2, (sub % 2) < 1
    q = [jnp.where(m4, tiles[a], tiles[a + 4]) + pltpu.roll(jnp.where(m4, tiles[a + 4], tiles[a]), 4, 0)
         for a in range(4)]
    r = [jnp.where(m2, q[a] + pltpu.roll(q[a], 6, 0), q[a + 1] + pltpu.roll(q[a + 1], 2, 0)) for a in (0, 2)]
    return jnp.where(m1, r[0] + pltpu.roll(r[0], 7, 0), r[1] + pltpu.roll(r[1], 1, 0))


_GROUP_ORDER = (0, 2, 1, 3, 4, 6, 5, 7)


def _peer_kernel(idx_ref, gate_ref, h_ref, x_ref, gt_ref, g_ref, uv_ref, y_ref, buf_ref, cb_ref, f_ref, sem_ref,
                 *, tt, n):
    i = pl.program_id(0)
    cblocks = D_MODEL // LANES
    groups = N_SEL // SUBLANES
    half = N_SEL // 2

    def issue(chunk, slot):
        t, p0 = chunk // 2, (chunk % 2) * half
        for p in range(p0, p0 + half):
            pltpu.make_async_copy(uv_ref.at[idx_ref[t, p]], buf_ref.at[slot, t * N_SEL + p],
                                  sem_ref.at[slot]).start(priority=p % 2)

    def run(fill, slot):
        do_issue = fill is not None
        if slot is None:
            for chunk in range(2 * tt):
                issue(chunk, fill)
            return
        pltpu.make_async_copy(buf_ref.at[slot], buf_ref.at[slot], sem_ref.at[slot]).wait()
        eye = (lax.broadcasted_iota(jnp.int32, (N_SEL, N_SEL), 0)
               == lax.broadcasted_iota(jnp.int32, (N_SEL, N_SEL), 1)).astype(F32)
        gate_t = _dot_nt(eye, gate_ref[...], HI)
        lane = lax.broadcasted_iota(jnp.int32, (N_SEL, tt), 1)
        hmat = jnp.zeros((N_SEL, tt), F32)
        for t in range(tt):
            if do_issue:
                issue(t, fill)
            x8 = jnp.concatenate([h_ref[t:t + 1, c * LANES:(c + 1) * LANES] for c in range(cblocks)], axis=0)
            sums = [_group_sums([buf_ref[slot, t * N_SEL + g * SUBLANES + j].astype(F32)[0:SUBLANES] * x8
                                 for j in _GROUP_ORDER]) for g in range(groups)]
            hcol = jnp.sum(jnp.concatenate(sums, axis=0), axis=-1, keepdims=True)
            hmat = jnp.where(lane == t, hcol, hmat)
        coef = gate_t * _gelu(hmat)
        for t in range(tt):
            if do_issue:
                issue(tt + t, fill)
            cb_ref[...] = jnp.broadcast_to(coef[:, t:t + 1], (N_SEL, LANES))
            parts = []
            for a in range(4):
                acc = cb_ref[a:a + 1, :] * buf_ref[slot, t * N_SEL + a].astype(F32)[SUBLANES:]
                for p in range(a + 4, N_SEL, 4):
                    acc = acc + cb_ref[p:p + 1, :] * buf_ref[slot, t * N_SEL + p].astype(F32)[SUBLANES:]
                parts.append(acc)
            f8 = (parts[0] + parts[1]) + (parts[2] + parts[3])
            for c in range(cblocks):
                f_ref[t:t + 1, c * LANES:(c + 1) * LANES] = f8[c:c + 1, :]
        y_ref[...] = x_ref[...] + gt_ref[...] * _rms(f_ref[...], g_ref[...])

    lag = PEER_SLOTS - 1
    fill = i % PEER_SLOTS
    drain = (i + 1) % PEER_SLOTS

    @pl.when(i < min(lag, n))
    def _():
        run(fill, None)

    if n > lag:
        for s in range(PEER_SLOTS):
            @pl.when(jnp.logical_and(jnp.logical_and(i >= lag, i < n), fill == s))
            def _():
                run(s, (s + 1) % PEER_SLOTS)

    @pl.when(i >= max(lag, n))
    def _():
        run(None, drain)


def _peer(eidx, gate, h2, x1, mods, mod_rows, tiles_per_group, tt, g_post, uv):
    t = h2.shape[0]
    n = t // tt
    lag = PEER_SLOTS - 1
    done = lambda i: (jnp.maximum(i - lag, 0), 0)
    return pl.pallas_call(
        functools.partial(_peer_kernel, tt=tt, n=n),
        grid=(n + lag,),
        in_specs=[pl.BlockSpec((tt, N_SEL), lambda i: (jnp.minimum(i, n - 1), 0), memory_space=pltpu.SMEM),
                  pl.BlockSpec((tt, N_SEL), done),
                  pl.BlockSpec((tt, D_MODEL), done),
                  pl.BlockSpec((tt, D_MODEL), done),
                  pl.BlockSpec((None, mod_rows, D_MODEL),
                               lambda i: (jnp.maximum(i - lag, 0) // tiles_per_group, 0, N_ADA - 1)),
                  pl.BlockSpec((1, D_MODEL), lambda i: (0, 0)),
                  pl.BlockSpec(memory_space=pl.ANY)],
        out_specs=pl.BlockSpec((tt, D_MODEL), done),
        out_shape=jax.ShapeDtypeStruct((t, D_MODEL), F32),
        scratch_shapes=[pltpu.VMEM((PEER_SLOTS, tt * N_SEL, ROW_TILE, LANES), BF16),
                        pltpu.VMEM((N_SEL, LANES), F32),
                        pltpu.VMEM((tt, D_MODEL), F32),
                        pltpu.SemaphoreType.DMA((PEER_SLOTS,))],
        compiler_params=_cparams("arbitrary"),
        name="peer",
    )(eidx, gate, h2, x1, mods, g_post, uv)


def _token_path(x, mods, mod_rows, tiles_per_group, tm, tt, mixer, p):
    attn, conv_in, small, z, ga, gb = _inproj(x, mods, mod_rows, tiles_per_group, tm, p["g_pre_mix"], p["w_in_parts"])
    o_att, o_d, extras = mixer(attn, conv_in, small, z)
    x1, h2 = _outproj(x, o_att, o_d, ga, gb, mods, mod_rows, tiles_per_group, tm, p["g_post_mix"], p["g_pre_ffn"],
                      p["w_att_out"], p["w_delta_out"], p["w_out"])
    eidx, gate = _route(h2, p["wq_t"], p["sub_keys"], tm)
    if mod_rows == 1:
        y = _peer(eidx, gate, h2, x1, mods, 1, tiles_per_group * tm // tt, tt, p["g_post_ffn"], p["uv"])
    else:
        y = _peer(eidx, gate, h2, x1, mods.reshape(-1, tt, N_ADA * D_MODEL), tt, 1, tt, p["g_post_ffn"], p["uv"])
    return y, extras


def _bias_tables(rel_bias):
    qi = jnp.arange(WINDOW, dtype=jnp.int32)[:, None]
    kj = jnp.arange(2 * WINDOW, dtype=jnp.int32)[None, :]
    dist = qi + WINDOW - kj
    valid = (dist >= 0) & (dist < WINDOW)
    bias = jnp.moveaxis(rel_bias[_rel_bucket(dist)], -1, 0)
    later = jnp.where(valid[None], bias, NEG_BIG)
    first = jnp.where((valid & (kj >= WINDOW))[None], bias, NEG_BIG)
    dist_s = WINDOW - 1 - jnp.arange(WINDOW, dtype=jnp.int32)
    bias_s = rel_bias[_rel_bucket(dist_s)].T
    return jnp.stack([first, later]), bias_s


def _layer(xp, xs, c_all, k_buf, v_buf, conv_buf, s0, rel_bias, lp):
    batch, seq, _ = xp.shape
    dec = xs.shape[0]
    assert xs.shape[1] == 1 and seq % WINDOW == 0 and k_buf.shape[1] == WINDOW

    ada = _ada(c_all, lp["w_ada"], lp["b_ada"])
    mods_p = ada[:batch].reshape(batch, 1, N_ADA * D_MODEL)
    mods_s = ada[batch:].reshape(1, dec, N_ADA * D_MODEL)

    w_in = lp["w_in"]
    cuts = (0, ATTN_W, ATTN_W + CONV_DIM, ATTN_W + CONV_DIM + 2 * N_DELTA_HEADS)
    cuts = cuts + (cuts[-1] + DELTA_W, cuts[-1] + DELTA_W + D_MODEL, cuts[-1] + DELTA_W + 2 * D_MODEL)
    parts = [w_in[:, a:b] for a, b in zip(cuts[:-1], cuts[1:])]
    parts[2] = jnp.pad(parts[2], ((0, 0), (0, LANES - 2 * N_DELTA_HEADS)))
    row = lambda v: v.reshape(1, -1)
    par = jnp.zeros((SUBLANES, LANES), F32).at[0, :N_DELTA_HEADS].set(lp["a_log"]).at[1, :N_DELTA_HEADS].set(lp["dt_bias"])
    p = dict(
        w_in_parts=[w.astype(BF16) for w in parts],
        g_pre_mix=row(lp["g_pre_mix"]), g_post_mix=row(lp["g_post_mix"]),
        g_pre_ffn=row(lp["g_pre_ffn"]), g_post_ffn=row(lp["g_post_ffn"]),
        w_att_out=lp["w_att_out"].astype(BF16), w_delta_out=lp["w_delta_out"].astype(BF16),
        w_out=lp["w_out"].astype(BF16),
        wq_t=lp["w_query"].T.astype(BF16), sub_keys=lp["sub_keys"],
        uv=jnp.concatenate([lp["expert_u"].astype(BF16), lp["expert_v"].astype(BF16)],
                           axis=1).reshape(-1, ROW_TILE, LANES),
    )
    norm = row(lp["delta_norm"])
    bias2, bias_s = _bias_tables(rel_bias)
    out = {}

    def mixer_prompt(attn, conv_in, small, z):
        o_att = _swa_prompt(attn, batch, seq, lp["attn_sinks"], bias2)
        nb = 2 if batch % 2 == 0 else 1
        o_d, s_new = _delta_prompt(conv_in.reshape(batch, seq, CONV_DIM), small.reshape(batch, seq, LANES),
                                   z.reshape(batch, seq, DELTA_W), batch, seq, lp["w_conv"], par, norm, nb)
        o_d = o_d.reshape(batch * seq, DELTA_W)
        a3 = attn.reshape(batch, seq, ATTN_W)
        new_k = a3[:, -WINDOW:, ATTN_Q_W:ATTN_Q_W + ATTN_KV_W].reshape(batch, WINDOW, N_KV_HEADS, HEAD_DIM)
        new_v = a3[:, -WINDOW:, ATTN_Q_W + ATTN_KV_W:].reshape(batch, WINDOW, N_KV_HEADS, HEAD_DIM)
        new_conv = conv_in.reshape(batch, seq, CONV_DIM)[:, -(CONV_W - 1):]
        return o_att, o_d, (new_k, new_v, new_conv, s_new)

    def mixer_sample(attn, conv_in, small, z):
        k_new = attn[:, ATTN_Q_W:ATTN_Q_W + ATTN_KV_W].reshape(dec, 1, N_KV_HEADS, HEAD_DIM)
        v_new = attn[:, ATTN_Q_W + ATTN_KV_W:].reshape(dec, 1, N_KV_HEADS, HEAD_DIM)
        new_k = jnp.concatenate([k_buf[:, 1:], k_new], axis=1)
        new_v = jnp.concatenate([v_buf[:, 1:], v_new], axis=1)
        sb = SUBLANES if dec % SUBLANES == 0 else dec
        o_att = _swa_sample(attn[:, :ATTN_Q_W].reshape(dec, N_Q_HEADS, HEAD_DIM),
                            new_k.reshape(dec, WINDOW, ATTN_KV_W), new_v.reshape(dec, WINDOW, ATTN_KV_W),
                            lp["attn_sinks"], bias_s, sb).reshape(dec, ATTN_Q_W)
        rows4 = [conv_buf[:, i] for i in range(CONV_W - 1)] + [conv_in]
        o_d, s_new = _delta_sample(rows4, small, z, s0, lp["w_conv"], par, norm, sb)
        new_conv = jnp.concatenate([conv_buf[:, 1:], conv_in[:, None]], axis=1)
        return o_att, o_d, (new_k, new_v, new_conv, s_new)

    tm_s = dec
    ys, ex_s = _token_path(xs.reshape(dec, D_MODEL), mods_s, dec, 1, tm_s, SUBLANES, mixer_sample, p)
    tm_p = 256 if seq % 256 == 0 else WINDOW
    yp, ex_p = _token_path(xp.reshape(batch * seq, D_MODEL), mods_p, 1, seq // tm_p, tm_p, SUBLANES, mixer_prompt, p)
    return yp.reshape(batch, seq, D_MODEL), ys.reshape(dec, 1, D_MODEL), ex_p, ex_s


def kernel(x_prompt, x_sample, cache_swa_k, cache_swa_v, state_conv, state_delta, c_prompt, c_sample, rel_bias, w_ada, b_ada, g_pre_mix, g_post_mix, g_pre_ffn, g_post_ffn, w_in, attn_sinks, w_conv, a_log, dt_bias, delta_norm, w_att_out, w_delta_out, w_out, w_query, sub_keys, expert_u, expert_v):
    depth = w_in.shape[0]
    yp, ys = x_prompt, x_sample
    c_all = jnp.concatenate([c_prompt, c_sample], axis=0)
    ex_p_all, ex_s_all = [], []
    for layer in range(depth):
        lp = dict(w_ada=w_ada[layer], b_ada=b_ada[layer], g_pre_mix=g_pre_mix[layer], g_post_mix=g_post_mix[layer],
                  g_pre_ffn=g_pre_ffn[layer], g_post_ffn=g_post_ffn[layer], w_in=w_in[layer],
                  attn_sinks=attn_sinks[layer], w_conv=w_conv[layer], a_log=a_log[layer], dt_bias=dt_bias[layer],
                  delta_norm=delta_norm[layer], w_att_out=w_att_out[layer], w_delta_out=w_delta_out[layer],
                  w_out=w_out[layer], w_query=w_query[layer], sub_keys=sub_keys[layer],
                  expert_u=expert_u[layer], expert_v=expert_v[layer])
        yp, ys, ex_p, ex_s = _layer(yp, ys, c_all, cache_swa_k[layer], cache_swa_v[layer], state_conv[layer],
                                    state_delta[layer], rel_bias, lp)
        ex_p_all.append(ex_p)
        ex_s_all.append(ex_s)
    stack = lambda exs, j: jnp.stack([e[j] for e in exs])
    return (yp, ys, stack(ex_p_all, 0), stack(ex_p_all, 1), stack(ex_p_all, 2), stack(ex_p_all, 3),
            stack(ex_s_all, 0), stack(ex_s_all, 1), stack(ex_s_all, 2), stack(ex_s_all, 3))
```
